```python
import jax, jax.numpy as jnp
from jax import lax
import numpy as np


D_MODEL = 1024
BATCH = 8
SEQ = 4096
DEPTH = 4

CTX_LEN = 256
GRID_W = 64
EPS = 1e-6
MIN_FORGET = 1e-20

D_MIX = D_MODEL
HG_HEADS = 4
HG_DIM = 64
HG_WIDTH = HG_HEADS * HG_DIM
HG_CHUNK = 64
MLA_HEADS = 6
MLA_NOPE = 64
MLA_ROPE = 32
MLA_V = 64
MLA_Q_RANK = 256
MLA_KV_RANK = 128
MLA_WIDTH = MLA_HEADS * MLA_V
NA_HEADS = 6
NA_DIM = 64
NA_WIDTH = NA_HEADS * NA_DIM
NA_WIN_R = 8
NA_WIN_C = 16
HG_IN = 5 * HG_WIDTH
MLA_IN = MLA_Q_RANK + MLA_KV_RANK + MLA_ROPE
NA_IN = 3 * NA_WIDTH
D_IN = HG_IN + MLA_IN + NA_IN
ROPE_BASE = 10000.0
Q_BLOCK = 128
N_EXPERTS = 16
EC_CAPACITY = 2
F_EXPERT = 1024

kernel_name = 'hybrid_hgrn2_mla_natten_ec_dit'


def _rmsnorm(x, g):
    x32 = x.astype(jnp.float32)
    y = x32 * lax.rsqrt(jnp.mean(x32 * x32, axis=-1, keepdims=True) + EPS)
    return (y * g.astype(jnp.float32)).astype(x.dtype)


def _split_heads(a, n_heads):
    B, L, W = a.shape
    return a.reshape(B, L, n_heads, W // n_heads).transpose(0, 2, 1, 3)


def _merge_heads(a):
    B, H, L, e = a.shape
    return a.transpose(0, 2, 1, 3).reshape(B, L, H * e)


def _rope_2d(x, n):
    t = jnp.arange(n)
    half = MLA_ROPE // 2
    inv = ROPE_BASE ** (-jnp.arange(0, half, 2, dtype=jnp.float32) / half)

    def rot(xa, pos):
        ang = pos.astype(jnp.float32)[:, None] * inv
        shape = (n,) + (1,) * (xa.ndim - 3) + (half // 2,)
        cos = jnp.cos(ang).reshape(shape).astype(xa.dtype)
        sin = jnp.sin(ang).reshape(shape).astype(xa.dtype)
        x1, x2 = xa[..., :half // 2], xa[..., half // 2:]
        return jnp.concatenate([x1 * cos - x2 * sin, x2 * cos + x1 * sin], axis=-1)

    return jnp.concatenate([rot(x[..., :half], t // GRID_W), rot(x[..., half:], t % GRID_W)], axis=-1)


def _softmax_attend(q, k, v, scale):
    s = jnp.einsum('bhqd,bhkd->bhqk', q, k).astype(jnp.float32) * scale
    p = jax.nn.softmax(s, axis=-1).astype(v.dtype)
    return jnp.einsum('bhqk,bhke->bhqe', p, v)


def _blocked_attend(q, k, v, scale):
    B, H, N, d = q.shape
    nb = N // Q_BLOCK
    qb = q.reshape(B, H, nb, Q_BLOCK, d).transpose(2, 0, 1, 3, 4)
    o = lax.map(lambda qq: _softmax_attend(qq, k, v, scale), qb)
    return o.transpose(1, 2, 0, 3, 4).reshape(B, H, N, v.shape[-1])


def _log_forget(z, lb):
    lb = lb.reshape(HG_HEADS, 1, HG_DIM).astype(jnp.float32)
    f = lb + (1.0 - lb) * jax.nn.sigmoid(z.astype(jnp.float32))
    return jnp.log(jnp.maximum(f, MIN_FORGET))


def _gla_chunk_scan(q, k, v, logf, s0):
    B, H, L, dk = q.shape
    dv = v.shape[-1]
    nc = L // HG_CHUNK

    def chunks(a):
        return a.reshape(B, H, nc, HG_CHUNK, a.shape[-1]).transpose(2, 0, 1, 3, 4)

    lower = jnp.tril(jnp.ones((HG_CHUNK, HG_CHUNK), dtype=bool))[..., None]

    def step(S, inp):
        qc, kc, vc, lf = inp
        b = jnp.cumsum(lf, axis=-2)
        diff = b[..., :, None, :] - b[..., None, :, :]
        decay = jnp.where(lower, jnp.exp(jnp.minimum(diff, 0.0)), 0.0).astype(qc.dtype)
        a = jnp.einsum('bhid,bhjd,bhijd->bhij', qc, kc, decay)
        o = (jnp.einsum('bhij,bhje->bhie', a, vc)
             + jnp.einsum('bhid,bhde->bhie', qc * jnp.exp(b).astype(qc.dtype), S))
        b_last = b[..., -1:, :]
        S = (S * jnp.exp(b_last)[..., 0, :, None].astype(S.dtype)
             + jnp.einsum('bhjd,bhje->bhde', kc * jnp.exp(b_last - b).astype(kc.dtype), vc))
        return S, o

    S, o = lax.scan(step, s0, (chunks(q), chunks(k), chunks(v), chunks(logf)))
    return o.transpose(1, 2, 0, 3, 4).reshape(B, H, L, dv), S


def _gla_final_state(k, v, logf):
    rest = lax.cumsum(logf, axis=2, reverse=True) - logf
    return jnp.einsum('bhjd,bhje->bhde', k * jnp.exp(jnp.minimum(rest, 0.0)).astype(k.dtype), v)


def _hgrn2_mixer(p_lat, p_ctx, lb, onorm_g, with_ctx_out):
    def prep(p):
        q, i, zf, zb, g = jnp.split(p, 5, axis=-1)
        q = _split_heads(q, HG_HEADS) * (HG_DIM ** -0.5)
        lf = (_log_forget(_split_heads(zf, HG_HEADS), lb[0]),
              _log_forget(_split_heads(zb, HG_HEADS), lb[1]))
        ks = (-jnp.expm1(lf[0]).astype(q.dtype), -jnp.expm1(lf[1]).astype(q.dtype))
        return q, _split_heads(i, HG_HEADS), lf, ks, g

    def flip(a):
        return jnp.flip(a, axis=2)

    def readout(o, g):
        return _merge_heads(_rmsnorm(o, onorm_g)) * jax.nn.silu(g)

    qc, ic, lfc, kc, gc = prep(p_ctx)
    ql, il, lfl, kl, gl = prep(p_lat)
    if with_ctx_out:
        zeros = jnp.zeros(qc.shape[:2] + (HG_DIM, HG_DIM), qc.dtype)
        oc_f, s_f = _gla_chunk_scan(qc, kc[0], ic, lfc[0], zeros)
        oc_b, s_b = _gla_chunk_scan(flip(qc), flip(kc[1]), flip(ic), flip(lfc[1]), zeros)
        o_ctx = readout(oc_f + flip(oc_b), gc)
    else:
        s_f = _gla_final_state(kc[0], ic, lfc[0])
        s_b = _gla_final_state(flip(kc[1]), flip(ic), flip(lfc[1]))
        o_ctx = None
    ol_f, _ = _gla_chunk_scan(ql, kl[0], il, lfl[0], s_f)
    ol_b, _ = _gla_chunk_scan(flip(ql), flip(kl[1]), flip(il), flip(lfl[1]), s_b)
    return readout(ol_f + flip(ol_b), gl), o_ctx


def _mla_mixer(p_lat, p_ctx, qnorm_g, w_uq, kvnorm_g, w_ukv, with_ctx_out):
    def project(p, rotate):
        B, L, _ = p.shape
        cq = p[..., :MLA_Q_RANK]
        ckv = p[..., MLA_Q_RANK:MLA_Q_RANK + MLA_KV_RANK]
        kr = p[..., MLA_Q_RANK + MLA_KV_RANK:]
        q = (_rmsnorm(cq, qnorm_g) @ w_uq).reshape(B, L, MLA_HEADS, MLA_NOPE + MLA_ROPE)
        kv = (_rmsnorm(ckv, kvnorm_g) @ w_ukv).reshape(B, L, MLA_HEADS, MLA_NOPE + MLA_V)
        q_nope, q_rope = q[..., :MLA_NOPE], q[..., MLA_NOPE:]
        k_nope, v = kv[..., :MLA_NOPE], kv[..., MLA_NOPE:]
        if rotate:
            q_rope = _rope_2d(q_rope, L)
            kr = _rope_2d(kr, L)
        k_rope = jnp.broadcast_to(kr[:, :, None, :], (B, L, MLA_HEADS, MLA_ROPE))
        q = jnp.concatenate([q_nope, q_rope], axis=-1).transpose(0, 2, 1, 3)
        k = jnp.concatenate([k_nope, k_rope], axis=-1).transpose(0, 2, 1, 3)
        return q, k, v.transpose(0, 2, 1, 3)

    scale = (MLA_NOPE + MLA_ROPE) ** -0.5
    ql, kl, vl = project(p_lat, True)
    qc, kc, vc = project(p_ctx, False)
    o_lat = _merge_heads(_blocked_attend(ql, jnp.concatenate([kl, kc], axis=2),
                                         jnp.concatenate([vl, vc], axis=2), scale))
    o_ctx = _merge_heads(_softmax_attend(qc, kc, vc, scale)) if with_ctx_out else None
    return o_lat, o_ctx


def _na_mixer(p_lat, p_ctx, rpb, with_ctx_out):
    ql, kl, vl = [_split_heads(a, NA_HEADS) for a in jnp.split(p_lat, 3, axis=-1)]
    qc, kc, vc = [_split_heads(a, NA_HEADS) for a in jnp.split(p_ctx, 3, axis=-1)]
    scale = NA_DIM ** -0.5
    B, H, N, d = ql.shape
    rows = N // GRID_W
    wr = min(NA_WIN_R, rows)
    kg = kl.reshape(B, H, rows, GRID_W, d)
    vg = vl.reshape(B, H, rows, GRID_W, d)
    cols = jnp.arange(GRID_W)
    col_idx = jnp.clip(cols - NA_WIN_C // 2, 0, GRID_W - NA_WIN_C)[:, None] + jnp.arange(NA_WIN_C)
    dcol = col_idx - cols[:, None] + (NA_WIN_C - 1)
    nwin = wr * NA_WIN_C

    def row_step(inp):
        r, q_r = inp
        rs = jnp.clip(r - wr // 2, 0, rows - wr)
        k_win = lax.dynamic_slice_in_dim(kg, rs, wr, axis=2)[:, :, :, col_idx]
        v_win = lax.dynamic_slice_in_dim(vg, rs, wr, axis=2)[:, :, :, col_idx]
        drow = rs + jnp.arange(wr) - r + (NA_WIN_R - 1)
        bias = rpb[:, drow[None, :, None], dcol[:, None, :]]
        s_win = (jnp.einsum('bhcd,bhrckd->bhcrk', q_r, k_win).astype(jnp.float32) * scale
                 + bias.astype(jnp.float32)).reshape(B, H, GRID_W, nwin)
        s_ctx = jnp.einsum('bhcd,bhjd->bhcj', q_r, kc).astype(jnp.float32) * scale
        p = jax.nn.softmax(jnp.concatenate([s_win, s_ctx], axis=-1), axis=-1).astype(vc.dtype)
        p_win = p[..., :nwin].reshape(B, H, GRID_W, wr, NA_WIN_C)
        return (jnp.einsum('bhcrk,bhrckd->bhcd', p_win, v_win)
                + jnp.einsum('bhcj,bhjd->bhcd', p[..., nwin:], vc))

    q_rows = ql.reshape(B, H, rows, GRID_W, d).transpose(2, 0, 1, 3, 4)
    o = lax.map(row_step, (jnp.arange(rows), q_rows))
    o_lat = o.transpose(1, 0, 3, 2, 4).reshape(B, N, H * d)
    o_ctx = _merge_heads(_softmax_attend(qc, kc, vc, scale)) if with_ctx_out else None
    return o_lat, o_ctx


def _ec_ffn(h, router_w, w1, w3, w2):
    B, L, D = h.shape
    cap = EC_CAPACITY * L // N_EXPERTS
    aff = jax.nn.softmax((h @ router_w).astype(jnp.float32), axis=-1)
    gate, idx = lax.top_k(aff.transpose(0, 2, 1), cap)
    xs = jax.vmap(lambda hb, ib: hb[ib])(h, idx)
    a = jnp.einsum('becd,edf->becf', xs, w1)
    u = jnp.einsum('becd,edf->becf', xs, w3)
    y = jnp.einsum('becf,efd->becd', jax.nn.silu(a) * u, w2) * gate[..., None].astype(h.dtype)
    return jax.vmap(lambda yb, ib: jnp.zeros((L, D), h.dtype).at[ib.reshape(-1)].add(yb.reshape(-1, D)))(y, idx)


def setup_inputs(seed: int = 0) -> dict:
    key = jax.random.key(seed)
    ks = jax.random.split(key, 22)

    def nrm(k, shape, scale):
        return jax.random.normal(k, shape, jnp.float32) * scale

    return {
        'x': nrm(ks[0], (BATCH, SEQ, D_MODEL), 1.0),
        'c': nrm(ks[1], (BATCH, D_MODEL), 1.0),
        'ctx': nrm(ks[2], (BATCH, CTX_LEN, D_MODEL), 1.0),
        'c_ctx': nrm(ks[3], (D_MODEL,), 1.0),
        'ada_w': nrm(ks[4], (DEPTH, D_MODEL, 6 * D_MODEL), 0.5 * D_MODEL ** -0.5),
        'ada_b': nrm(ks[5], (DEPTH, 6 * D_MODEL), 0.02),
        'norm_mix_g': 1.0 + nrm(ks[6], (DEPTH, D_MODEL), 0.02),
        'norm_ffn_g': 1.0 + nrm(ks[7], (DEPTH, D_MODEL), 0.02),
        'w_in': nrm(ks[8], (DEPTH, D_MODEL, D_IN), D_MODEL ** -0.5),
        'hgrn_lb_logits': nrm(ks[9], (2, DEPTH, HG_WIDTH), 0.1),
        'hgrn_onorm_g': 1.0 + nrm(ks[10], (DEPTH, HG_DIM), 0.02),
        'mla_qnorm_g': 1.0 + nrm(ks[11], (DEPTH, MLA_Q_RANK), 0.02),
        'mla_w_uq': nrm(ks[12], (DEPTH, MLA_Q_RANK, MLA_HEADS * (MLA_NOPE + MLA_ROPE)), MLA_Q_RANK ** -0.5),
        'mla_kvnorm_g': 1.0 + nrm(ks[13], (DEPTH, MLA_KV_RANK), 0.02),
        'mla_w_ukv': nrm(ks[14], (DEPTH, MLA_KV_RANK, MLA_HEADS * (MLA_NOPE + MLA_V)), MLA_KV_RANK ** -0.5),
        'na_rpb': nrm(ks[15], (DEPTH, NA_HEADS, 2 * NA_WIN_R - 1, 2 * NA_WIN_C - 1), 0.1),
        'w_out': nrm(ks[16], (DEPTH, D_MIX, D_MODEL), D_MIX ** -0.5),
        'router_w': nrm(ks[17], (DEPTH, D_MODEL, N_EXPERTS), D_MODEL ** -0.5),
        'exp_w1': nrm(ks[18], (DEPTH, N_EXPERTS, D_MODEL, F_EXPERT), D_MODEL ** -0.5),
        'exp_w3': nrm(ks[19], (DEPTH, N_EXPERTS, D_MODEL, F_EXPERT), D_MODEL ** -0.5),
        'exp_w2': nrm(ks[20], (DEPTH, N_EXPERTS, F_EXPERT, D_MODEL), F_EXPERT ** -0.5),
        'final_norm_g': 1.0 + nrm(ks[21], (D_MODEL,), 0.02),
    }


def reference(x, c, ctx, c_ctx, ada_w, ada_b, norm_mix_g, norm_ffn_g, w_in, hgrn_lb_logits,
              hgrn_onorm_g, mla_qnorm_g, mla_w_uq, mla_kvnorm_g, mla_w_ukv, na_rpb, w_out,
              router_w, exp_w1, exp_w3, exp_w2, final_norm_g):
    lb_w = jax.nn.softmax(hgrn_lb_logits.astype(jnp.float32), axis=1)
    lb_all = jnp.cumsum(lb_w, axis=1) - lb_w[:, :1]
    s_lat = jax.nn.silu(c)
    s_ctx = jax.nn.silu(c_ctx)
    for layer in range(DEPTH):
        with_ctx_out = layer < DEPTH - 1
        mod_l = (s_lat @ ada_w[layer] + ada_b[layer])[:, None, :]
        mod_c = s_ctx @ ada_w[layer] + ada_b[layer]
        sh_a, sc_a, g_a, sh_f, sc_f, g_f = jnp.split(mod_l, 6, axis=-1)
        csh_a, csc_a, cg_a, csh_f, csc_f, cg_f = jnp.split(mod_c, 6, axis=-1)

        h_l = _rmsnorm(x, norm_mix_g[layer]) * (1.0 + sc_a) + sh_a
        h_c = _rmsnorm(ctx, norm_mix_g[layer]) * (1.0 + csc_a) + csh_a
        p_l = h_l @ w_in[layer]
        p_c = h_c @ w_in[layer]
        a0, a1 = HG_IN, HG_IN + MLA_IN
        hg_l, hg_c = _hgrn2_mixer(p_l[..., :a0], p_c[..., :a0], lb_all[:, layer],
                                  hgrn_onorm_g[layer], with_ctx_out)
        ml_l, ml_c = _mla_mixer(p_l[..., a0:a1], p_c[..., a0:a1], mla_qnorm_g[layer], mla_w_uq[layer],
                                mla_kvnorm_g[layer], mla_w_ukv[layer], with_ctx_out)
        na_l, na_c = _na_mixer(p_l[..., a1:], p_c[..., a1:], na_rpb[layer], with_ctx_out)
        x = x + g_a * (jnp.concatenate([hg_l, ml_l, na_l], axis=-1) @ w_out[layer])

        h = _rmsnorm(x, norm_ffn_g[layer]) * (1.0 + sc_f) + sh_f
        x = x + g_f * _ec_ffn(h, router_w[layer], exp_w1[layer], exp_w3[layer], exp_w2[layer])

        if with_ctx_out:
            ctx = ctx + cg_a * (jnp.concatenate([hg_c, ml_c, na_c], axis=-1) @ w_out[layer])
            hc = _rmsnorm(ctx, norm_ffn_g[layer]) * (1.0 + csc_f) + csh_f
            ctx = ctx + cg_f * _ec_ffn(hc, router_w[layer], exp_w1[layer], exp_w3[layer], exp_w2[layer])
    return _rmsnorm(x, final_norm_g)
```

```python
import functools

import jax
import jax.numpy as jnp
from jax import lax
from jax.experimental import pallas as pl
from jax.experimental.pallas import tpu as pltpu

D_MODEL = 1024
BATCH = 8
SEQ = 4096
DEPTH = 4

CTX_LEN = 256
GRID_W = 64
EPS = 1e-6
MIN_FORGET = 1e-20

D_MIX = D_MODEL
HG_HEADS = 4
HG_DIM = 64
HG_WIDTH = HG_HEADS * HG_DIM
HG_CHUNK = 64
MLA_HEADS = 6
MLA_NOPE = 64
MLA_ROPE = 32
MLA_V = 64
MLA_Q_RANK = 256
MLA_KV_RANK = 128
MLA_WIDTH = MLA_HEADS * MLA_V
NA_HEADS = 6
NA_DIM = 64
NA_WIDTH = NA_HEADS * NA_DIM
NA_WIN_R = 8
NA_WIN_C = 16
HG_IN = 5 * HG_WIDTH
MLA_IN = MLA_Q_RANK + MLA_KV_RANK + MLA_ROPE
NA_IN = 3 * NA_WIDTH
D_IN = HG_IN + MLA_IN + NA_IN
ROPE_BASE = 10000.0
Q_BLOCK = 128
N_EXPERTS = 16
EC_CAPACITY = 2
F_EXPERT = 1024

VMEM_LIMIT_BYTES = 56 * 1024 * 1024


def _mm_body(a_ref, w_ref, o_ref):
    o_ref[...] = jnp.dot(a_ref[...].astype(jnp.bfloat16), w_ref[...],
                         preferred_element_type=jnp.float32)


def _mm(a, w, tm=512):
    M, K = a.shape
    N = w.shape[1]
    tm = min(tm, M)
    assert M % tm == 0
    return pl.pallas_call(
        _mm_body,
        grid=(M // tm,),
        in_specs=[pl.BlockSpec((tm, K), lambda i: (i, 0)),
                  pl.BlockSpec((K, N), lambda i: (0, 0))],
        out_specs=pl.BlockSpec((tm, N), lambda i: (i, 0)),
        out_shape=jax.ShapeDtypeStruct((M, N), jnp.float32),
        compiler_params=pltpu.CompilerParams(
            dimension_semantics=("arbitrary",), vmem_limit_bytes=VMEM_LIMIT_BYTES),
        name="mm",
    )(a, w)


def _proj(h, w_bf16):
    B, L, K = h.shape
    return _mm(h.reshape(B * L, K), w_bf16).reshape(B, L, w_bf16.shape[1])


def _rmsnorm(x, g):
    x32 = x.astype(jnp.float32)
    y = x32 * lax.rsqrt(jnp.mean(x32 * x32, axis=-1, keepdims=True) + EPS)
    return (y * g.astype(jnp.float32)).astype(x.dtype)


def _split_heads(a, n_heads):
    B, L, W = a.shape
    return a.reshape(B, L, n_heads, W // n_heads).transpose(0, 2, 1, 3)


def _merge_heads(a):
    B, H, L, e = a.shape
    return a.transpose(0, 2, 1, 3).reshape(B, L, H * e)


def _rope_2d(x, n):
    t = jnp.arange(n)
    half = MLA_ROPE // 2
    inv = ROPE_BASE ** (-jnp.arange(0, half, 2, dtype=jnp.float32) / half)

    def rot(xa, pos):
        ang = pos.astype(jnp.float32)[:, None] * inv
        shape = (n,) + (1,) * (xa.ndim - 3) + (half // 2,)
        cos = jnp.cos(ang).reshape(shape).astype(xa.dtype)
        sin = jnp.sin(ang).reshape(shape).astype(xa.dtype)
        x1, x2 = xa[..., :half // 2], xa[..., half // 2:]
        return jnp.concatenate([x1 * cos - x2 * sin, x2 * cos + x1 * sin], axis=-1)

    return jnp.concatenate([rot(x[..., :half], t // GRID_W), rot(x[..., half:], t % GRID_W)], axis=-1)


def _softmax_attend(q, k, v, scale):
    s = jnp.einsum('bhqd,bhkd->bhqk', q, k).astype(jnp.float32) * scale
    p = jax.nn.softmax(s, axis=-1).astype(v.dtype)
    return jnp.einsum('bhqk,bhke->bhqe', p, v)


def _blocked_attend(q, k, v, scale):
    B, H, N, d = q.shape
    nb = N // Q_BLOCK
    qb = q.reshape(B, H, nb, Q_BLOCK, d).transpose(2, 0, 1, 3, 4)
    o = lax.map(lambda qq: _softmax_attend(qq, k, v, scale), qb)
    return o.transpose(1, 2, 0, 3, 4).reshape(B, H, N, v.shape[-1])


def _log_forget(z, lb):
    lb = lb.reshape(HG_HEADS, 1, HG_DIM).astype(jnp.float32)
    f = lb + (1.0 - lb) * jax.nn.sigmoid(z.astype(jnp.float32))
    return jnp.log(jnp.maximum(f, MIN_FORGET))


def _gla_chunk_scan(q, k, v, logf, s0):
    B, H, L, dk = q.shape
    dv = v.shape[-1]
    nc = L // HG_CHUNK

    def chunks(a):
        return a.reshape(B, H, nc, HG_CHUNK, a.shape[-1]).transpose(2, 0, 1, 3, 4)

    lower = jnp.tril(jnp.ones((HG_CHUNK, HG_CHUNK), dtype=bool))[..., None]

    def step(S, inp):
        qc, kc, vc, lf = inp
        b = jnp.cumsum(lf, axis=-2)
        diff = b[..., :, None, :] - b[..., None, :, :]
        decay = jnp.where(lower, jnp.exp(jnp.minimum(diff, 0.0)), 0.0).astype(qc.dtype)
        a = jnp.einsum('bhid,bhjd,bhijd->bhij', qc, kc, decay)
        o = (jnp.einsum('bhij,bhje->bhie', a, vc)
             + jnp.einsum('bhid,bhde->bhie', qc * jnp.exp(b).astype(qc.dtype), S))
        b_last = b[..., -1:, :]
        S = (S * jnp.exp(b_last)[..., 0, :, None].astype(S.dtype)
             + jnp.einsum('bhjd,bhje->bhde', kc * jnp.exp(b_last - b).astype(kc.dtype), vc))
        return S, o

    S, o = lax.scan(step, s0, (chunks(q), chunks(k), chunks(v), chunks(logf)))
    return o.transpose(1, 2, 0, 3, 4).reshape(B, H, L, dv), S


def _gla_final_state(k, v, logf):
    rest = lax.cumsum(logf, axis=2, reverse=True) - logf
    return jnp.einsum('bhjd,bhje->bhde', k * jnp.exp(jnp.minimum(rest, 0.0)).astype(k.dtype), v)


def _hgrn2_mixer(p_lat, p_ctx, lb, onorm_g, with_ctx_out):
    def prep(p):
        q, i, zf, zb, g = jnp.split(p, 5, axis=-1)
        q = _split_heads(q, HG_HEADS) * (HG_DIM ** -0.5)
        lf = (_log_forget(_split_heads(zf, HG_HEADS), lb[0]),
              _log_forget(_split_heads(zb, HG_HEADS), lb[1]))
        ks = (-jnp.expm1(lf[0]).astype(q.dtype), -jnp.expm1(lf[1]).astype(q.dtype))
        return q, _split_heads(i, HG_HEADS), lf, ks, g

    def flip(a):
        return jnp.flip(a, axis=2)

    def readout(o, g):
        return _merge_heads(_rmsnorm(o, onorm_g)) * jax.nn.silu(g)

    qc, ic, lfc, kc, gc = prep(p_ctx)
    ql, il, lfl, kl, gl = prep(p_lat)
    if with_ctx_out:
        zeros = jnp.zeros(qc.shape[:2] + (HG_DIM, HG_DIM), qc.dtype)
        oc_f, s_f = _gla_chunk_scan(qc, kc[0], ic, lfc[0], zeros)
        oc_b, s_b = _gla_chunk_scan(flip(qc), flip(kc[1]), flip(ic), flip(lfc[1]), zeros)
        o_ctx = readout(oc_f + flip(oc_b), gc)
    else:
        s_f = _gla_final_state(kc[0], ic, lfc[0])
        s_b = _gla_final_state(flip(kc[1]), flip(ic), flip(lfc[1]))
        o_ctx = None
    ol_f, _ = _gla_chunk_scan(ql, kl[0], il, lfl[0], s_f)
    ol_b, _ = _gla_chunk_scan(flip(ql), flip(kl[1]), flip(il), flip(lfl[1]), s_b)
    return readout(ol_f + flip(ol_b), gl), o_ctx


def _mla_mixer(p_lat, p_ctx, qnorm_g, w_uq, kvnorm_g, w_ukv, with_ctx_out):
    def project(p, rotate):
        B, L, _ = p.shape
        cq = p[..., :MLA_Q_RANK]
        ckv = p[..., MLA_Q_RANK:MLA_Q_RANK + MLA_KV_RANK]
        kr = p[..., MLA_Q_RANK + MLA_KV_RANK:]
        q = (_rmsnorm(cq, qnorm_g) @ w_uq).reshape(B, L, MLA_HEADS, MLA_NOPE + MLA_ROPE)
        kv = (_rmsnorm(ckv, kvnorm_g) @ w_ukv).reshape(B, L, MLA_HEADS, MLA_NOPE + MLA_V)
        q_nope, q_rope = q[..., :MLA_NOPE], q[..., MLA_NOPE:]
        k_nope, v = kv[..., :MLA_NOPE], kv[..., MLA_NOPE:]
        if rotate:
            q_rope = _rope_2d(q_rope, L)
            kr = _rope_2d(kr, L)
        k_rope = jnp.broadcast_to(kr[:, :, None, :], (B, L, MLA_HEADS, MLA_ROPE))
        q = jnp.concatenate([q_nope, q_rope], axis=-1).transpose(0, 2, 1, 3)
        k = jnp.concatenate([k_nope, k_rope], axis=-1).transpose(0, 2, 1, 3)
        return q, k, v.transpose(0, 2, 1, 3)

    scale = (MLA_NOPE + MLA_ROPE) ** -0.5
    ql, kl, vl = project(p_lat, True)
    qc, kc, vc = project(p_ctx, False)
    o_lat = _merge_heads(_blocked_attend(ql, jnp.concatenate([kl, kc], axis=2),
                                         jnp.concatenate([vl, vc], axis=2), scale))
    o_ctx = _merge_heads(_softmax_attend(qc, kc, vc, scale)) if with_ctx_out else None
    return o_lat, o_ctx


def _na_mixer(p_lat, p_ctx, rpb, with_ctx_out):
    ql, kl, vl = [_split_heads(a, NA_HEADS) for a in jnp.split(p_lat, 3, axis=-1)]
    qc, kc, vc = [_split_heads(a, NA_HEADS) for a in jnp.split(p_ctx, 3, axis=-1)]
    scale = NA_DIM ** -0.5
    B, H, N, d = ql.shape
    rows = N // GRID_W
    wr = min(NA_WIN_R, rows)
    kg = kl.reshape(B, H, rows, GRID_W, d)
    vg = vl.reshape(B, H, rows, GRID_W, d)
    cols = jnp.arange(GRID_W)
    col_idx = jnp.clip(cols - NA_WIN_C // 2, 0, GRID_W - NA_WIN_C)[:, None] + jnp.arange(NA_WIN_C)
    dcol = col_idx - cols[:, None] + (NA_WIN_C - 1)
    nwin = wr * NA_WIN_C

    def row_step(inp):
        r, q_r = inp
        rs = jnp.clip(r - wr // 2, 0, rows - wr)
        k_win = lax.dynamic_slice_in_dim(kg, rs, wr, axis=2)[:, :, :, col_idx]
        v_win = lax.dynamic_slice_in_dim(vg, rs, wr, axis=2)[:, :, :, col_idx]
        drow = rs + jnp.arange(wr) - r + (NA_WIN_R - 1)
        bias = rpb[:, drow[None, :, None], dcol[:, None, :]]
        s_win = (jnp.einsum('bhcd,bhrckd->bhcrk', q_r, k_win).astype(jnp.float32) * scale
                 + bias.astype(jnp.float32)).reshape(B, H, GRID_W, nwin)
        s_ctx = jnp.einsum('bhcd,bhjd->bhcj', q_r, kc).astype(jnp.float32) * scale
        p = jax.nn.softmax(jnp.concatenate([s_win, s_ctx], axis=-1), axis=-1).astype(vc.dtype)
        p_win = p[..., :nwin].reshape(B, H, GRID_W, wr, NA_WIN_C)
        return (jnp.einsum('bhcrk,bhrckd->bhcd', p_win, v_win)
                + jnp.einsum('bhcj,bhjd->bhcd', p[..., nwin:], vc))

    q_rows = ql.reshape(B, H, rows, GRID_W, d).transpose(2, 0, 1, 3, 4)
    o = lax.map(row_step, (jnp.arange(rows), q_rows))
    o_lat = o.transpose(1, 0, 3, 2, 4).reshape(B, N, H * d)
    o_ctx = _merge_heads(_softmax_attend(qc, kc, vc, scale)) if with_ctx_out else None
    return o_lat, o_ctx


def _ec_ffn(h, router_w, w1, w3, w2):
    B, L, D = h.shape
    cap = EC_CAPACITY * L // N_EXPERTS
    aff = jax.nn.softmax((h @ router_w).astype(jnp.float32), axis=-1)
    gate, idx = lax.top_k(aff.transpose(0, 2, 1), cap)
    xs = jax.vmap(lambda hb, ib: hb[ib])(h, idx)
    a = jnp.einsum('becd,edf->becf', xs, w1)
    u = jnp.einsum('becd,edf->becf', xs, w3)
    y = jnp.einsum('becf,efd->becd', jax.nn.silu(a) * u, w2) * gate[..., None].astype(h.dtype)
    return jax.vmap(lambda yb, ib: jnp.zeros((L, D), h.dtype).at[ib.reshape(-1)].add(yb.reshape(-1, D)))(y, idx)


def kernel(x, c, ctx, c_ctx, ada_w, ada_b, norm_mix_g, norm_ffn_g, w_in, hgrn_lb_logits,
           hgrn_onorm_g, mla_qnorm_g, mla_w_uq, mla_kvnorm_g, mla_w_ukv, na_rpb, w_out,
           router_w, exp_w1, exp_w3, exp_w2, final_norm_g):
    lb_w = jax.nn.softmax(hgrn_lb_logits.astype(jnp.float32), axis=1)
    lb_all = jnp.cumsum(lb_w, axis=1) - lb_w[:, :1]
    s_lat = jax.nn.silu(c)
    s_ctx = jax.nn.silu(c_ctx)
    w_in_b = w_in.astype(jnp.bfloat16)
    w_out_b = w_out.astype(jnp.bfloat16)
    for layer in range(DEPTH):
        with_ctx_out = layer < DEPTH - 1
        mod_l = (s_lat @ ada_w[layer] + ada_b[layer])[:, None, :]
        mod_c = s_ctx @ ada_w[layer] + ada_b[layer]
        sh_a, sc_a, g_a, sh_f, sc_f, g_f = jnp.split(mod_l, 6, axis=-1)
        csh_a, csc_a, cg_a, csh_f, csc_f, cg_f = jnp.split(mod_c, 6, axis=-1)

        h_l = _rmsnorm(x, norm_mix_g[layer]) * (1.0 + sc_a) + sh_a
        h_c = _rmsnorm(ctx, norm_mix_g[layer]) * (1.0 + csc_a) + csh_a
        p_l = _proj(h_l, w_in_b[layer])
        p_c = _proj(h_c, w_in_b[layer])
        a0, a1 = HG_IN, HG_IN + MLA_IN
        hg_l, hg_c = _hgrn2_mixer(p_l[..., :a0], p_c[..., :a0], lb_all[:, layer],
                                  hgrn_onorm_g[layer], with_ctx_out)
        ml_l, ml_c = _mla_mixer(p_l[..., a0:a1], p_c[..., a0:a1], mla_qnorm_g[layer], mla_w_uq[layer],
                                mla_kvnorm_g[layer], mla_w_ukv[layer], with_ctx_out)
        na_l, na_c = _na_mixer(p_l[..., a1:], p_c[..., a1:], na_rpb[layer], with_ctx_out)
        x = x + g_a * _proj(jnp.concatenate([hg_l, ml_l, na_l], axis=-1), w_out_b[layer])

        h = _rmsnorm(x, norm_ffn_g[layer]) * (1.0 + sc_f) + sh_f
        x = x + g_f * _ec_ffn(h, router_w[layer], exp_w1[layer], exp_w3[layer], exp_w2[layer])

        if with_ctx_out:
            ctx = ctx + cg_a * _proj(jnp.concatenate([hg_c, ml_c, na_c], axis=-1), w_out_b[layer])
            hc = _rmsnorm(ctx, norm_ffn_g[layer]) * (1.0 + csc_f) + csh_f
            ctx = ctx + cg_f * _ec_ffn(hc, router_w[layer], exp_w1[layer], exp_w3[layer], exp_w2[layer])
    return _rmsnorm(x, final_norm_g)
```

```python
import functools

import numpy as np
import jax
import jax.numpy as jnp
from jax import lax
from jax.experimental import pallas as pl
from jax.experimental.pallas import tpu as pltpu

D_MODEL = 1024
BATCH = 8
SEQ = 4096
DEPTH = 4

CTX_LEN = 256
GRID_W = 64
EPS = 1e-6
MIN_FORGET = 1e-20

D_MIX = D_MODEL
HG_HEADS = 4
HG_DIM = 64
HG_WIDTH = HG_HEADS * HG_DIM
HG_CHUNK = 64
MLA_HEADS = 6
MLA_NOPE = 64
MLA_ROPE = 32
MLA_V = 64
MLA_Q_RANK = 256
MLA_KV_RANK = 128
MLA_WIDTH = MLA_HEADS * MLA_V
NA_HEADS = 6
NA_DIM = 64
NA_WIDTH = NA_HEADS * NA_DIM
NA_WIN_R = 8
NA_WIN_C = 16
HG_IN = 5 * HG_WIDTH
MLA_IN = MLA_Q_RANK + MLA_KV_RANK + MLA_ROPE
NA_IN = 3 * NA_WIDTH
D_IN = HG_IN + MLA_IN + NA_IN
ROPE_BASE = 10000.0
Q_BLOCK = 128
N_EXPERTS = 16
EC_CAPACITY = 2
F_EXPERT = 1024

VMEM_LIMIT_BYTES = 56 * 1024 * 1024


def _mm_body(a_ref, w_ref, o_ref):
    o_ref[...] = jnp.dot(a_ref[...].astype(jnp.bfloat16), w_ref[...],
                         preferred_element_type=jnp.float32)


def _mm(a, w, tm=512):
    M, K = a.shape
    N = w.shape[1]
    tm = min(tm, M)
    assert M % tm == 0
    return pl.pallas_call(
        _mm_body,
        grid=(M // tm,),
        in_specs=[pl.BlockSpec((tm, K), lambda i: (i, 0)),
                  pl.BlockSpec((K, N), lambda i: (0, 0))],
        out_specs=pl.BlockSpec((tm, N), lambda i: (i, 0)),
        out_shape=jax.ShapeDtypeStruct((M, N), jnp.float32),
        compiler_params=pltpu.CompilerParams(
            dimension_semantics=("arbitrary",), vmem_limit_bytes=VMEM_LIMIT_BYTES),
        name="mm",
    )(a, w)


GRID_H = SEQ // GRID_W
NA_QROWS = 2
NA_QB = NA_QROWS * GRID_W
NA_KROWS = NA_QROWS + NA_WIN_R - 1
NA_KB = NA_KROWS * GRID_W
NA_NBLK = GRID_H // NA_QROWS
NA_MASKED = -1e30
LANES = 128


def _na_window_start(m):
    return min(max(NA_QROWS * m - NA_WIN_R // 2, 0), GRID_H - NA_KROWS)


@functools.lru_cache(maxsize=None)
def _na_patterns():
    qi = np.arange(NA_QB)
    kj = np.arange(NA_KB)
    c = (qi % GRID_W)[:, None]
    kc = (kj % GRID_W)[None, :]
    cs = np.clip(c - NA_WIN_C // 2, 0, GRID_W - NA_WIN_C)
    tables, pat_of_block = [], []
    for m in range(NA_NBLK):
        r = (NA_QROWS * m + qi // GRID_W)[:, None]
        kr = (_na_window_start(m) + kj // GRID_W)[None, :]
        rs = np.clip(r - NA_WIN_R // 2, 0, GRID_H - NA_WIN_R)
        valid = (kr >= rs) & (kr < rs + NA_WIN_R) & (kc >= cs) & (kc < cs + NA_WIN_C)
        idx = np.where(valid, (kr - r + NA_WIN_R - 1) * (2 * NA_WIN_C - 1) + (kc - c + NA_WIN_C - 1), 0)
        key = (idx.tobytes(), valid.tobytes())
        for p, (k2, _, _) in enumerate(tables):
            if k2 == key:
                pat_of_block.append(p)
                break
        else:
            pat_of_block.append(len(tables))
            tables.append((key, idx, valid))
    idx = np.stack([t[1] for t in tables]).astype(np.int32)
    valid = np.stack([t[2] for t in tables])
    return idx, valid, tuple(pat_of_block)


def _na_bias_tables(rpb):
    idx, valid, _ = _na_patterns()
    flat = rpb.reshape(NA_HEADS, -1).astype(jnp.float32)
    return jnp.where(valid[None], flat[:, idx], NA_MASKED)


def _dot_nt(a, b):
    return lax.dot_general(a, b, (((1,), (1,)), ((), ())), preferred_element_type=jnp.float32)


def _dot(a, b):
    return jnp.dot(a, b, preferred_element_type=jnp.float32)


def _na_body(q_ref, k_ref, v_ref, qc_ref, kc_ref, vc_ref, bias_ref, o_ref, oc_ref):
    _, _, pat_of_block = _na_patterns()
    lane = lax.broadcasted_iota(jnp.int32, (1, LANES), 1)
    head_masks = [(lane >= NA_DIM * hh) & (lane < NA_DIM * (hh + 1)) for hh in range(2)]
    kc = kc_ref[...]
    vc = vc_ref[...]
    vc_h = [jnp.where(hm, vc, jnp.zeros_like(vc)) for hm in head_masks]

    def block(m, ws, pat):
        q = q_ref[pl.ds(pl.multiple_of(m * NA_QB, NA_QB), NA_QB), :]
        kstart = pl.multiple_of(ws * GRID_W, GRID_W)
        kwin = k_ref[pl.ds(kstart, NA_KB), :]
        vwin = v_ref[pl.ds(kstart, NA_KB), :]
        out = None
        for hh in range(2):
            qh = jnp.where(head_masks[hh], q, jnp.zeros_like(q))
            s_w = _dot_nt(qh, kwin) + bias_ref[hh, pat]
            s_c = _dot_nt(qh, kc)
            mx = jnp.maximum(jnp.max(s_w, axis=-1, keepdims=True), jnp.max(s_c, axis=-1, keepdims=True))
            p_w = jnp.exp(s_w - mx)
            p_c = jnp.exp(s_c - mx)
            den = jnp.sum(p_w, axis=-1, keepdims=True) + jnp.sum(p_c, axis=-1, keepdims=True)
            vh = jnp.where(head_masks[hh], vwin, jnp.zeros_like(vwin))
            o = _dot(p_w.astype(jnp.bfloat16), vh) + _dot(p_c.astype(jnp.bfloat16), vc_h[hh])
            o = o / den
            out = o if out is None else out + o
        o_ref[pl.ds(pl.multiple_of(m * NA_QB, NA_QB), NA_QB), :] = out.astype(o_ref.dtype)

    interior = [m for m in range(NA_NBLK) if pat_of_block.count(pat_of_block[m]) > 1]
    lo, hi = interior[0], interior[-1] + 1
    assert interior == list(range(lo, hi)) and len({pat_of_block[m] for m in interior}) == 1
    for m in list(range(lo)) + list(range(hi, NA_NBLK)):
        block(m, _na_window_start(m), pat_of_block[m])

    def loop_body(m, carry):
        block(m, NA_QROWS * m - NA_WIN_R // 2, pat_of_block[lo])
        return carry

    lax.fori_loop(lo, hi, loop_body, 0)

    qc = qc_ref[...]
    out = None
    for hh in range(2):
        qh = jnp.where(head_masks[hh], qc, jnp.zeros_like(qc))
        s = _dot_nt(qh, kc)
        p = jnp.exp(s - jnp.max(s, axis=-1, keepdims=True))
        o = _dot(p.astype(jnp.bfloat16), vc_h[hh]) / jnp.sum(p, axis=-1, keepdims=True)
        out = o if out is None else out + o
    oc_ref[...] = out.astype(oc_ref.dtype)


def _na_attention(q, k, v, qc, kc, vc, bias):
    B = q.shape[0]
    n_pat = bias.shape[1]
    lat = pl.BlockSpec((None, SEQ, LANES), lambda b, hp: (b, 0, hp))
    cx = pl.BlockSpec((None, CTX_LEN, LANES), lambda b, hp: (b, 0, hp))
    return pl.pallas_call(
        _na_body,
        grid=(B, NA_WIDTH // LANES),
        in_specs=[lat, lat, lat, cx, cx, cx,
                  pl.BlockSpec((2, n_pat, NA_QB, NA_KB), lambda b, hp: (hp, 0, 0, 0))],
        out_specs=[lat, cx],
        out_shape=[jax.ShapeDtypeStruct((B, SEQ, NA_WIDTH), jnp.bfloat16),
                   jax.ShapeDtypeStruct((B, CTX_LEN, NA_WIDTH), jnp.bfloat16)],
        compiler_params=pltpu.CompilerParams(
            dimension_semantics=("arbitrary", "arbitrary"), vmem_limit_bytes=VMEM_LIMIT_BYTES),
        name="na_attention",
    )(q, k, v, qc, kc, vc, bias)


def _proj(h, w_bf16):
    B, L, K = h.shape
    return _mm(h.reshape(B * L, K), w_bf16).reshape(B, L, w_bf16.shape[1])


def _rmsnorm(x, g):
    x32 = x.astype(jnp.float32)
    y = x32 * lax.rsqrt(jnp.mean(x32 * x32, axis=-1, keepdims=True) + EPS)
    return (y * g.astype(jnp.float32)).astype(x.dtype)


def _split_heads(a, n_heads):
    B, L, W = a.shape
    return a.reshape(B, L, n_heads, W // n_heads).transpose(0, 2, 1, 3)


def _merge_heads(a):
    B, H, L, e = a.shape
    return a.transpose(0, 2, 1, 3).reshape(B, L, H * e)


def _rope_2d(x, n):
    t = jnp.arange(n)
    half = MLA_ROPE // 2
    inv = ROPE_BASE ** (-jnp.arange(0, half, 2, dtype=jnp.float32) / half)

    def rot(xa, pos):
        ang = pos.astype(jnp.float32)[:, None] * inv
        shape = (n,) + (1,) * (xa.ndim - 3) + (half // 2,)
        cos = jnp.cos(ang).reshape(shape).astype(xa.dtype)
        sin = jnp.sin(ang).reshape(shape).astype(xa.dtype)
        x1, x2 = xa[..., :half // 2], xa[..., half // 2:]
        return jnp.concatenate([x1 * cos - x2 * sin, x2 * cos + x1 * sin], axis=-1)

    return jnp.concatenate([rot(x[..., :half], t // GRID_W), rot(x[..., half:], t % GRID_W)], axis=-1)


def _softmax_attend(q, k, v, scale):
    s = jnp.einsum('bhqd,bhkd->bhqk', q, k).astype(jnp.float32) * scale
    p = jax.nn.softmax(s, axis=-1).astype(v.dtype)
    return jnp.einsum('bhqk,bhke->bhqe', p, v)


def _blocked_attend(q, k, v, scale):
    B, H, N, d = q.shape
    nb = N // Q_BLOCK
    qb = q.reshape(B, H, nb, Q_BLOCK, d).transpose(2, 0, 1, 3, 4)
    o = lax.map(lambda qq: _softmax_attend(qq, k, v, scale), qb)
    return o.transpose(1, 2, 0, 3, 4).reshape(B, H, N, v.shape[-1])


def _log_forget(z, lb):
    lb = lb.reshape(HG_HEADS, 1, HG_DIM).astype(jnp.float32)
    f = lb + (1.0 - lb) * jax.nn.sigmoid(z.astype(jnp.float32))
    return jnp.log(jnp.maximum(f, MIN_FORGET))


def _gla_chunk_scan(q, k, v, logf, s0):
    B, H, L, dk = q.shape
    dv = v.shape[-1]
    nc = L // HG_CHUNK

    def chunks(a):
        return a.reshape(B, H, nc, HG_CHUNK, a.shape[-1]).transpose(2, 0, 1, 3, 4)

    lower = jnp.tril(jnp.ones((HG_CHUNK, HG_CHUNK), dtype=bool))[..., None]

    def step(S, inp):
        qc, kc, vc, lf = inp
        b = jnp.cumsum(lf, axis=-2)
        diff = b[..., :, None, :] - b[..., None, :, :]
        decay = jnp.where(lower, jnp.exp(jnp.minimum(diff, 0.0)), 0.0).astype(qc.dtype)
        a = jnp.einsum('bhid,bhjd,bhijd->bhij', qc, kc, decay)
        o = (jnp.einsum('bhij,bhje->bhie', a, vc)
             + jnp.einsum('bhid,bhde->bhie', qc * jnp.exp(b).astype(qc.dtype), S))
        b_last = b[..., -1:, :]
        S = (S * jnp.exp(b_last)[..., 0, :, None].astype(S.dtype)
             + jnp.einsum('bhjd,bhje->bhde', kc * jnp.exp(b_last - b).astype(kc.dtype), vc))
        return S, o

    S, o = lax.scan(step, s0, (chunks(q), chunks(k), chunks(v), chunks(logf)))
    return o.transpose(1, 2, 0, 3, 4).reshape(B, H, L, dv), S


def _gla_final_state(k, v, logf):
    rest = lax.cumsum(logf, axis=2, reverse=True) - logf
    return jnp.einsum('bhjd,bhje->bhde', k * jnp.exp(jnp.minimum(rest, 0.0)).astype(k.dtype), v)


def _hgrn2_mixer(p_lat, p_ctx, lb, onorm_g, with_ctx_out):
    def prep(p):
        q, i, zf, zb, g = jnp.split(p, 5, axis=-1)
        q = _split_heads(q, HG_HEADS) * (HG_DIM ** -0.5)
        lf = (_log_forget(_split_heads(zf, HG_HEADS), lb[0]),
              _log_forget(_split_heads(zb, HG_HEADS), lb[1]))
        ks = (-jnp.expm1(lf[0]).astype(q.dtype), -jnp.expm1(lf[1]).astype(q.dtype))
        return q, _split_heads(i, HG_HEADS), lf, ks, g

    def flip(a):
        return jnp.flip(a, axis=2)

    def readout(o, g):
        return _merge_heads(_rmsnorm(o, onorm_g)) * jax.nn.silu(g)

    qc, ic, lfc, kc, gc = prep(p_ctx)
    ql, il, lfl, kl, gl = prep(p_lat)
    if with_ctx_out:
        zeros = jnp.zeros(qc.shape[:2] + (HG_DIM, HG_DIM), qc.dtype)
        oc_f, s_f = _gla_chunk_scan(qc, kc[0], ic, lfc[0], zeros)
        oc_b, s_b = _gla_chunk_scan(flip(qc), flip(kc[1]), flip(ic), flip(lfc[1]), zeros)
        o_ctx = readout(oc_f + flip(oc_b), gc)
    else:
        s_f = _gla_final_state(kc[0], ic, lfc[0])
        s_b = _gla_final_state(flip(kc[1]), flip(ic), flip(lfc[1]))
        o_ctx = None
    ol_f, _ = _gla_chunk_scan(ql, kl[0], il, lfl[0], s_f)
    ol_b, _ = _gla_chunk_scan(flip(ql), flip(kl[1]), flip(il), flip(lfl[1]), s_b)
    return readout(ol_f + flip(ol_b), gl), o_ctx


def _mla_mixer(p_lat, p_ctx, qnorm_g, w_uq, kvnorm_g, w_ukv, with_ctx_out):
    def project(p, rotate):
        B, L, _ = p.shape
        cq = p[..., :MLA_Q_RANK]
        ckv = p[..., MLA_Q_RANK:MLA_Q_RANK + MLA_KV_RANK]
        kr = p[..., MLA_Q_RANK + MLA_KV_RANK:]
        q = (_rmsnorm(cq, qnorm_g) @ w_uq).reshape(B, L, MLA_HEADS, MLA_NOPE + MLA_ROPE)
        kv = (_rmsnorm(ckv, kvnorm_g) @ w_ukv).reshape(B, L, MLA_HEADS, MLA_NOPE + MLA_V)
        q_nope, q_rope = q[..., :MLA_NOPE], q[..., MLA_NOPE:]
        k_nope, v = kv[..., :MLA_NOPE], kv[..., MLA_NOPE:]
        if rotate:
            q_rope = _rope_2d(q_rope, L)
            kr = _rope_2d(kr, L)
        k_rope = jnp.broadcast_to(kr[:, :, None, :], (B, L, MLA_HEADS, MLA_ROPE))
        q = jnp.concatenate([q_nope, q_rope], axis=-1).transpose(0, 2, 1, 3)
        k = jnp.concatenate([k_nope, k_rope], axis=-1).transpose(0, 2, 1, 3)
        return q, k, v.transpose(0, 2, 1, 3)

    scale = (MLA_NOPE + MLA_ROPE) ** -0.5
    ql, kl, vl = project(p_lat, True)
    qc, kc, vc = project(p_ctx, False)
    o_lat = _merge_heads(_blocked_attend(ql, jnp.concatenate([kl, kc], axis=2),
                                         jnp.concatenate([vl, vc], axis=2), scale))
    o_ctx = _merge_heads(_softmax_attend(qc, kc, vc, scale)) if with_ctx_out else None
    return o_lat, o_ctx


def _na_mixer(p_lat, p_ctx, rpb):
    def split(p):
        q, k, v = jnp.split(p, 3, axis=-1)
        return ((q * NA_DIM ** -0.5).astype(jnp.bfloat16), k.astype(jnp.bfloat16), v.astype(jnp.bfloat16))

    return _na_attention(*split(p_lat), *split(p_ctx), _na_bias_tables(rpb))


def _ec_ffn(h, router_w, w1, w3, w2):
    B, L, D = h.shape
    cap = EC_CAPACITY * L // N_EXPERTS
    aff = jax.nn.softmax((h @ router_w).astype(jnp.float32), axis=-1)
    gate, idx = lax.top_k(aff.transpose(0, 2, 1), cap)
    xs = jax.vmap(lambda hb, ib: hb[ib])(h, idx)
    a = jnp.einsum('becd,edf->becf', xs, w1)
    u = jnp.einsum('becd,edf->becf', xs, w3)
    y = jnp.einsum('becf,efd->becd', jax.nn.silu(a) * u, w2) * gate[..., None].astype(h.dtype)
    return jax.vmap(lambda yb, ib: jnp.zeros((L, D), h.dtype).at[ib.reshape(-1)].add(yb.reshape(-1, D)))(y, idx)


def kernel(x, c, ctx, c_ctx, ada_w, ada_b, norm_mix_g, norm_ffn_g, w_in, hgrn_lb_logits,
           hgrn_onorm_g, mla_qnorm_g, mla_w_uq, mla_kvnorm_g, mla_w_ukv, na_rpb, w_out,
           router_w, exp_w1, exp_w3, exp_w2, final_norm_g):
    lb_w = jax.nn.softmax(hgrn_lb_logits.astype(jnp.float32), axis=1)
    lb_all = jnp.cumsum(lb_w, axis=1) - lb_w[:, :1]
    s_lat = jax.nn.silu(c)
    s_ctx = jax.nn.silu(c_ctx)
    w_in_b = w_in.astype(jnp.bfloat16)
    w_out_b = w_out.astype(jnp.bfloat16)
    for layer in range(DEPTH):
        with_ctx_out = layer < DEPTH - 1
        mod_l = (s_lat @ ada_w[layer] + ada_b[layer])[:, None, :]
        mod_c = s_ctx @ ada_w[layer] + ada_b[layer]
        sh_a, sc_a, g_a, sh_f, sc_f, g_f = jnp.split(mod_l, 6, axis=-1)
        csh_a, csc_a, cg_a, csh_f, csc_f, cg_f = jnp.split(mod_c, 6, axis=-1)

        h_l = _rmsnorm(x, norm_mix_g[layer]) * (1.0 + sc_a) + sh_a
        h_c = _rmsnorm(ctx, norm_mix_g[layer]) * (1.0 + csc_a) + csh_a
        p_l = _proj(h_l, w_in_b[layer])
        p_c = _proj(h_c, w_in_b[layer])
        a0, a1 = HG_IN, HG_IN + MLA_IN
        hg_l, hg_c = _hgrn2_mixer(p_l[..., :a0], p_c[..., :a0], lb_all[:, layer],
                                  hgrn_onorm_g[layer], with_ctx_out)
        ml_l, ml_c = _mla_mixer(p_l[..., a0:a1], p_c[..., a0:a1], mla_qnorm_g[layer], mla_w_uq[layer],
                                mla_kvnorm_g[layer], mla_w_ukv[layer], with_ctx_out)
        na_l, na_c = _na_mixer(p_l[..., a1:], p_c[..., a1:], na_rpb[layer])
        x = x + g_a * _proj(jnp.concatenate([hg_l, ml_l, na_l], axis=-1), w_out_b[layer])

        h = _rmsnorm(x, norm_ffn_g[layer]) * (1.0 + sc_f) + sh_f
        x = x + g_f * _ec_ffn(h, router_w[layer], exp_w1[layer], exp_w3[layer], exp_w2[layer])

        if with_ctx_out:
            ctx = ctx + cg_a * _proj(jnp.concatenate([hg_c, ml_c, na_c], axis=-1), w_out_b[layer])
            hc = _rmsnorm(ctx, norm_ffn_g[layer]) * (1.0 + csc_f) + csh_f
            ctx = ctx + cg_f * _ec_ffn(hc, router_w[layer], exp_w1[layer], exp_w3[layer], exp_w2[layer])
    return _rmsnorm(x, final_norm_g)
```

```python
import functools
import math

import numpy as np
import jax
import jax.numpy as jnp
from jax import lax
from jax.experimental import pallas as pl
from jax.experimental.pallas import tpu as pltpu

D_MODEL = 1024
BATCH = 8
SEQ = 4096
DEPTH = 4

CTX_LEN = 256
GRID_W = 64
EPS = 1e-6
MIN_FORGET = 1e-20

D_MIX = D_MODEL
HG_HEADS = 4
HG_DIM = 64
HG_WIDTH = HG_HEADS * HG_DIM
HG_CHUNK = 64
MLA_HEADS = 6
MLA_NOPE = 64
MLA_ROPE = 32
MLA_V = 64
MLA_Q_RANK = 256
MLA_KV_RANK = 128
MLA_WIDTH = MLA_HEADS * MLA_V
NA_HEADS = 6
NA_DIM = 64
NA_WIDTH = NA_HEADS * NA_DIM
NA_WIN_R = 8
NA_WIN_C = 16
HG_IN = 5 * HG_WIDTH
MLA_IN = MLA_Q_RANK + MLA_KV_RANK + MLA_ROPE
NA_IN = 3 * NA_WIDTH
D_IN = HG_IN + MLA_IN + NA_IN
ROPE_BASE = 10000.0
N_EXPERTS = 16
EC_CAPACITY = 2
F_EXPERT = 1024

LANES = 128
SUBLANES = 8
VMEM_LIMIT_BYTES = 56 * 1024 * 1024

TOK = CTX_LEN + SEQ
ROW_TILE = 256
N_ROW_TILES = TOK // ROW_TILE
CTX_TILES = CTX_LEN // ROW_TILE

MLA_HEAD_PAD = LANES
P32_HG = 0
P32_CQ = HG_IN
P32_CKV = P32_CQ + MLA_Q_RANK
P32_KR = P32_CKV + MLA_KV_RANK
P32_KRP = P32_KR + LANES
P32_WIDTH = P32_KRP + LANES
P_WIDTH = P32_WIDTH + NA_IN


def _cparams(n_grid_dims):
    return pltpu.CompilerParams(dimension_semantics=("arbitrary",) * n_grid_dims,
                                vmem_limit_bytes=VMEM_LIMIT_BYTES)


def _dot_nt(a, b):
    return lax.dot_general(a, b, (((1,), (1,)), ((), ())), preferred_element_type=jnp.float32)


def _dot(a, b):
    return jnp.dot(a, b, preferred_element_type=jnp.float32)


def _kind_of_tile(t):
    return (t >= CTX_TILES).astype(jnp.int32)


def _in_proj_body(x_ref, g_ref, mod_ref, w_ref, p32_ref, p16_ref):
    x = x_ref[...]
    y = x * lax.rsqrt(jnp.mean(x * x, axis=-1, keepdims=True) + EPS) * g_ref[...]
    h = y * (1.0 + mod_ref[1:2, :]) + mod_ref[0:1, :]
    p = _dot(h.astype(jnp.bfloat16), w_ref[...])
    p32_ref[...] = p[:, :P32_WIDTH]
    p16_ref[...] = p[:, P32_WIDTH:].astype(p16_ref.dtype)


def _in_proj(xc, g, mods, w):
    B = xc.shape[0]
    return pl.pallas_call(
        _in_proj_body,
        grid=(B, N_ROW_TILES),
        in_specs=[pl.BlockSpec((None, ROW_TILE, D_MODEL), lambda b, t: (b, t, 0)),
                  pl.BlockSpec((1, D_MODEL), lambda b, t: (0, 0)),
                  pl.BlockSpec((None, None, 6, D_MODEL), lambda b, t: (b, _kind_of_tile(t), 0, 0)),
                  pl.BlockSpec((D_MODEL, P_WIDTH), lambda b, t: (0, 0))],
        out_specs=[pl.BlockSpec((None, ROW_TILE, P32_WIDTH), lambda b, t: (b, t, 0)),
                   pl.BlockSpec((None, ROW_TILE, NA_IN), lambda b, t: (b, t, 0))],
        out_shape=[jax.ShapeDtypeStruct((B, TOK, P32_WIDTH), jnp.float32),
                   jax.ShapeDtypeStruct((B, TOK, NA_IN), jnp.bfloat16)],
        compiler_params=_cparams(2),
        name="in_proj",
    )(xc, g, mods, w)


def _rope_partner():
    half = MLA_ROPE // 2
    quarter = half // 2
    partner, sign = [], []
    for r in range(MLA_ROPE):
        base, j = (r // half) * half, r % half
        partner.append(base + (j + quarter if j < quarter else j - quarter))
        sign.append(-1.0 if j < quarter else 1.0)
    return np.array(partner), np.array(sign, np.float32)


def _widen_w_in(w_in):
    L = w_in.shape[0]
    a0, a1 = HG_IN, HG_IN + MLA_IN
    partner, sign = _rope_partner()
    kr = w_in[..., a0 + MLA_Q_RANK + MLA_KV_RANK:a1]
    zl = jnp.zeros((L, D_MODEL, MLA_NOPE), w_in.dtype)
    zr = jnp.zeros((L, D_MODEL, MLA_HEAD_PAD - MLA_NOPE - MLA_ROPE), w_in.dtype)
    na = w_in[..., a1:]
    return jnp.concatenate([
        w_in[..., :a0 + MLA_Q_RANK + MLA_KV_RANK],
        zl, kr, zr,
        zl, kr[..., partner] * sign, zr,
        na[..., :NA_WIDTH] * NA_DIM ** -0.5, na[..., NA_WIDTH:]], axis=-1).astype(jnp.bfloat16)


GRID_H = SEQ // GRID_W
NA_QROWS = 2
NA_QB = NA_QROWS * GRID_W
NA_KROWS = NA_QROWS + NA_WIN_R - 1
NA_KB = NA_KROWS * GRID_W
NA_NBLK = GRID_H // NA_QROWS
NA_MASKED = -1e30


def _na_window_start(m):
    return min(max(NA_QROWS * m - NA_WIN_R // 2, 0), GRID_H - NA_KROWS)


@functools.lru_cache(maxsize=None)
def _na_patterns():
    qi = np.arange(NA_QB)
    kj = np.arange(NA_KB)
    c = (qi % GRID_W)[:, None]
    kc = (kj % GRID_W)[None, :]
    cs = np.clip(c - NA_WIN_C // 2, 0, GRID_W - NA_WIN_C)
    tables, pat_of_block = [], []
    for m in range(NA_NBLK):
        r = (NA_QROWS * m + qi // GRID_W)[:, None]
        kr = (_na_window_start(m) + kj // GRID_W)[None, :]
        rs = np.clip(r - NA_WIN_R // 2, 0, GRID_H - NA_WIN_R)
        valid = (kr >= rs) & (kr < rs + NA_WIN_R) & (kc >= cs) & (kc < cs + NA_WIN_C)
        idx = np.where(valid, (kr - r + NA_WIN_R - 1) * (2 * NA_WIN_C - 1) + (kc - c + NA_WIN_C - 1), 0)
        key = (idx.tobytes(), valid.tobytes())
        for p, (k2, _, _) in enumerate(tables):
            if k2 == key:
                pat_of_block.append(p)
                break
        else:
            pat_of_block.append(len(tables))
            tables.append((key, idx, valid))
    idx = np.stack([t[1] for t in tables]).astype(np.int32)
    valid = np.stack([t[2] for t in tables])
    return idx, valid, tuple(pat_of_block)


def _na_bias_tables(rpb):
    idx, valid, _ = _na_patterns()
    flat = rpb.reshape(NA_HEADS, -1).astype(jnp.float32)
    return jnp.where(valid[None], flat[:, idx], NA_MASKED)


def _na_body(q_ref, k_ref, v_ref, bias_ref, o_ref):
    _, _, pat_of_block = _na_patterns()
    lane = lax.broadcasted_iota(jnp.int32, (1, LANES), 1)
    head_masks = [(lane >= NA_DIM * hh) & (lane < NA_DIM * (hh + 1)) for hh in range(2)]
    kc = k_ref[0:CTX_LEN, :]
    vc = v_ref[0:CTX_LEN, :]
    vc_h = [jnp.where(hm, vc, jnp.zeros_like(vc)) for hm in head_masks]

    def block(m, ws, pat):
        qrows = pl.ds(pl.multiple_of(CTX_LEN + m * NA_QB, NA_QB), NA_QB)
        q = q_ref[qrows, :]
        krows = pl.ds(pl.multiple_of(CTX_LEN + ws * GRID_W, GRID_W), NA_KB)
        kwin = k_ref[krows, :]
        vwin = v_ref[krows, :]
        out = None
        for hh in range(2):
            qh = jnp.where(head_masks[hh], q, jnp.zeros_like(q))
            s_w = _dot_nt(qh, kwin) + bias_ref[hh, pat]
            s_c = _dot_nt(qh, kc)
            mx = jnp.maximum(jnp.max(s_w, axis=-1, keepdims=True), jnp.max(s_c, axis=-1, keepdims=True))
            p_w = jnp.exp(s_w - mx)
            p_c = jnp.exp(s_c - mx)
            den = jnp.sum(p_w, axis=-1, keepdims=True) + jnp.sum(p_c, axis=-1, keepdims=True)
            vh = jnp.where(head_masks[hh], vwin, jnp.zeros_like(vwin))
            o = _dot(p_w.astype(jnp.bfloat16), vh) + _dot(p_c.astype(jnp.bfloat16), vc_h[hh])
            o = o / den
            out = o if out is None else out + o
        o_ref[qrows, :] = out.astype(o_ref.dtype)

    interior = [m for m in range(NA_NBLK) if pat_of_block.count(pat_of_block[m]) > 1]
    lo, hi = interior[0], interior[-1] + 1
    assert interior == list(range(lo, hi)) and len({pat_of_block[m] for m in interior}) == 1
    for m in list(range(lo)) + list(range(hi, NA_NBLK)):
        block(m, _na_window_start(m), pat_of_block[m])

    def loop_body(m, carry):
        block(m, NA_QROWS * m - NA_WIN_R // 2, pat_of_block[lo])
        return carry

    lax.fori_loop(lo, hi, loop_body, 0)

    qc = q_ref[0:CTX_LEN, :]
    out = None
    for hh in range(2):
        qh = jnp.where(head_masks[hh], qc, jnp.zeros_like(qc))
        s = _dot_nt(qh, kc)
        p = jnp.exp(s - jnp.max(s, axis=-1, keepdims=True))
        o = _dot(p.astype(jnp.bfloat16), vc_h[hh]) / jnp.sum(p, axis=-1, keepdims=True)
        out = o if out is None else out + o
    o_ref[0:CTX_LEN, :] = out.astype(o_ref.dtype)


def _na_attention(p16, bias):
    B = p16.shape[0]
    n_pat = bias.shape[1]
    n_pairs = NA_WIDTH // LANES

    def col(group):
        return pl.BlockSpec((None, TOK, LANES), lambda b, hp, group=group: (b, 0, group * n_pairs + hp))

    return pl.pallas_call(
        _na_body,
        grid=(B, n_pairs),
        in_specs=[col(0), col(1), col(2),
                  pl.BlockSpec((2, n_pat, NA_QB, NA_KB), lambda b, hp: (hp, 0, 0, 0))],
        out_specs=col(0),
        out_shape=jax.ShapeDtypeStruct((B, TOK, NA_WIDTH), jnp.bfloat16),
        compiler_params=_cparams(2),
        name="na_attention",
    )(p16, p16, p16, bias)


HG_NCHUNK = TOK // HG_CHUNK
HG_CTX_CHUNKS = CTX_LEN // HG_CHUNK


def _split3_bf16(x):
    hi = x.astype(jnp.bfloat16)
    r1 = x - hi.astype(jnp.float32)
    mid = r1.astype(jnp.bfloat16)
    lo = (r1 - mid.astype(jnp.float32)).astype(jnp.bfloat16)
    return hi, mid, lo


def _hgrn_body(q_ref, i_ref, zf_ref, zb_ref, g_ref, lb_ref, gn_ref, o_ref,
               of_s, b_s, k_s, q_s, v_s, oi_s):
    C = HG_CHUNK
    r_i = lax.broadcasted_iota(jnp.int32, (C, C), 0)
    c_i = lax.broadcasted_iota(jnp.int32, (C, C), 1)
    tri = {True: (c_i <= r_i).astype(jnp.bfloat16), False: (c_i >= r_i).astype(jnp.bfloat16)}
    r2 = lax.broadcasted_iota(jnp.int32, (LANES, LANES), 0) // HG_DIM
    c2 = lax.broadcasted_iota(jnp.int32, (LANES, LANES), 1) // HG_DIM
    same_head = r2 == c2
    half_ones = same_head.astype(jnp.bfloat16)
    row_id = lax.broadcasted_iota(jnp.int32, (C, LANES), 0)
    lane_lo = lax.broadcasted_iota(jnp.int32, (1, LANES), 1) < HG_DIM

    def chunk(c_idx, s_t, fwd):
        rows = pl.ds(pl.multiple_of(c_idx * C, C), C)
        lb = lb_ref[0:1, :] if fwd else lb_ref[1:2, :]
        z = (zf_ref if fwd else zb_ref)[rows, :]
        f = jnp.maximum(lb + (1.0 - lb) * jax.nn.sigmoid(z), MIN_FORGET)
        lf = jnp.log(f)
        hi, mid, lo = _split3_bf16(lf)
        b = _dot(tri[fwd], hi) + _dot(tri[fwd], mid) + _dot(tri[fwd], lo)
        q = q_ref[rows, :] * (HG_DIM ** -0.5)
        v = i_ref[rows, :]
        b_s[...] = b
        k_s[...] = 1.0 - f
        q_s[...] = q
        v_s[...] = v
        b_last = b[C - 1:C, :] if fwd else b[0:1, :]

        for i in range(C):
            g0, g1 = (0, i // SUBLANES + 1) if fwd else (i // SUBLANES, C // SUBLANES)
            blk = slice(g0 * SUBLANES, g1 * SUBLANES)
            e = jnp.exp(jnp.minimum(b_s[i:i + 1, :] - b_s[blk, :], 0.0))
            cmat = e * k_s[blk, :] * q_s[i:i + 1, :]
            visible = (row_id[blk, :] <= i) if fwd else (row_id[blk, :] >= i)
            cmat = jnp.where(visible, cmat, 0.0)
            a_b = _dot(cmat.astype(jnp.bfloat16), half_ones)
            oi_s[i:i + 1, :] = jnp.sum(a_b * v_s[blk, :], axis=0, keepdims=True)

        qe = (q * jnp.exp(b)).astype(jnp.bfloat16)
        o = oi_s[...] + _dot_nt(qe, s_t.astype(jnp.bfloat16))
        ke = (k_s[...] * jnp.exp(b_last - b)).astype(jnp.bfloat16)
        upd = _dot(v.T.astype(jnp.bfloat16), ke)
        s_t = s_t * jnp.exp(b_last) + jnp.where(same_head, upd, 0.0)
        return rows, o, s_t

    def fwd_body(n, s_t):
        rows, o, s_t = chunk(n, s_t, True)
        of_s[rows, :] = o
        return s_t

    def bwd_body(n, s_t):
        c_idx = jnp.where(n < HG_CTX_CHUNKS, HG_CTX_CHUNKS - 1 - n, HG_NCHUNK + HG_CTX_CHUNKS - 1 - n)
        rows, o, s_t = chunk(c_idx, s_t, False)
        o = o + of_s[rows, :]
        sq = o * o
        ms = jnp.where(lane_lo,
                       jnp.sum(jnp.where(lane_lo, sq, 0.0), axis=-1, keepdims=True),
                       jnp.sum(jnp.where(lane_lo, 0.0, sq), axis=-1, keepdims=True)) * (1.0 / HG_DIM)
        y = o * lax.rsqrt(ms + EPS) * gn_ref[...]
        g = g_ref[rows, :]
        o_ref[rows, :] = (y * (g * jax.nn.sigmoid(g))).astype(o_ref.dtype)
        return s_t

    zero = jnp.zeros((LANES, LANES), jnp.float32)
    lax.fori_loop(0, HG_NCHUNK, fwd_body, zero)
    lax.fori_loop(0, HG_NCHUNK, bwd_body, zero)


def _hgrn_mixer(p32, lb, onorm_g):
    B = p32.shape[0]
    n_pairs = HG_WIDTH // LANES

    def col(group):
        return pl.BlockSpec((None, TOK, LANES), lambda b, hp, group=group: (b, 0, group * n_pairs + hp))

    gn = jnp.tile(onorm_g.astype(jnp.float32), LANES // HG_DIM).reshape(1, LANES)
    chunk_scratch = pltpu.VMEM((HG_CHUNK, LANES), jnp.float32)
    return pl.pallas_call(
        _hgrn_body,
        grid=(B, n_pairs),
        in_specs=[col(0), col(1), col(2), col(3), col(4),
                  pl.BlockSpec((2, LANES), lambda b, hp: (0, hp)),
                  pl.BlockSpec((1, LANES), lambda b, hp: (0, 0))],
        out_specs=pl.BlockSpec((None, TOK, LANES), lambda b, hp: (b, 0, hp)),
        out_shape=jax.ShapeDtypeStruct((B, TOK, HG_WIDTH), jnp.bfloat16),
        scratch_shapes=[pltpu.VMEM((TOK, LANES), jnp.float32)] + [chunk_scratch] * 5,
        compiler_params=_cparams(2),
        name="hgrn2",
    )(p32, p32, p32, p32, p32, lb.astype(jnp.float32), gn)


MLA_QK_WIDTH = MLA_HEADS * MLA_HEAD_PAD
MLA_QSCALE = (MLA_NOPE + MLA_ROPE) ** -0.5 * math.log2(math.e)
MLA_TQ = 256


def _mla_prep_body(cq_ref, ckv_ref, kr_ref, krp_ref, cos_ref, sin_ref, gq_ref, gkv_ref,
                   wq_ref, wqr_ref, wk_ref, wv_ref, q_out, k_out, v_out):
    def rms(x, g):
        return (x * lax.rsqrt(jnp.mean(x * x, axis=-1, keepdims=True) + EPS) * g).astype(jnp.bfloat16)

    cos = cos_ref[...]
    sin = sin_ref[...]
    cqn = rms(cq_ref[...], gq_ref[...])
    q = _dot(cqn, wq_ref[...])
    qr = _dot(cqn, wqr_ref[...])
    ckvn = rms(ckv_ref[...], gkv_ref[...])
    kn = _dot(ckvn, wk_ref[...])
    v_out[...] = _dot(ckvn, wv_ref[...]).astype(v_out.dtype)
    k_rope = kr_ref[...] * cos + krp_ref[...] * sin
    for h in range(MLA_HEADS):
        sl = slice(h * MLA_HEAD_PAD, (h + 1) * MLA_HEAD_PAD)
        q_out[:, sl] = ((q[:, sl] * cos + qr[:, sl] * sin) * MLA_QSCALE).astype(q_out.dtype)
        k_out[:, sl] = (kn[:, sl] + k_rope).astype(k_out.dtype)


def _mla_weights(w_uq, w_ukv):
    partner, sign = _rope_partner()
    L = w_uq.shape[0]
    wq = w_uq.reshape(L, MLA_Q_RANK, MLA_HEADS, MLA_NOPE + MLA_ROPE)
    q_nope, q_rope = wq[..., :MLA_NOPE], wq[..., MLA_NOPE:]
    pad = jnp.zeros(wq.shape[:3] + (MLA_HEAD_PAD - MLA_NOPE - MLA_ROPE,), w_uq.dtype)
    wq_pad = jnp.concatenate([q_nope, q_rope, pad], axis=-1)
    wq_rot = jnp.concatenate([jnp.zeros_like(q_nope), q_rope[..., partner] * sign, pad], axis=-1)
    wkv = w_ukv.reshape(L, MLA_KV_RANK, MLA_HEADS, MLA_NOPE + MLA_V)
    k_nope, v = wkv[..., :MLA_NOPE], wkv[..., MLA_NOPE:]
    wk_pad = jnp.concatenate([k_nope, jnp.zeros(k_nope.shape[:3] + (MLA_HEAD_PAD - MLA_NOPE,), w_ukv.dtype)], axis=-1)
    bf = lambda a, n: a.reshape(L, a.shape[1], n).astype(jnp.bfloat16)
    return bf(wq_pad, MLA_QK_WIDTH), bf(wq_rot, MLA_QK_WIDTH), bf(wk_pad, MLA_QK_WIDTH), bf(v, MLA_WIDTH)


def _rope_tables():
    t = jnp.arange(SEQ)
    half = MLA_ROPE // 2
    inv = ROPE_BASE ** (-jnp.arange(0, half, 2, dtype=jnp.float32) / half)
    ang_r = (t // GRID_W).astype(jnp.float32)[:, None] * inv
    ang_c = (t % GRID_W).astype(jnp.float32)[:, None] * inv
    ang = jnp.concatenate([ang_r, ang_r, ang_c, ang_c], axis=-1)
    ones = jnp.ones((SEQ, MLA_NOPE), jnp.float32)
    zeros_n = jnp.zeros((SEQ, MLA_NOPE), jnp.float32)
    zeros_p = jnp.zeros((SEQ, MLA_HEAD_PAD - MLA_NOPE - MLA_ROPE), jnp.float32)
    cos_l = jnp.concatenate([ones, jnp.cos(ang), zeros_p], axis=-1)
    sin_l = jnp.concatenate([zeros_n, jnp.sin(ang), zeros_p], axis=-1)
    lane = np.arange(MLA_HEAD_PAD)
    cos_c = jnp.broadcast_to(jnp.asarray((lane < MLA_NOPE + MLA_ROPE).astype(np.float32)), (CTX_LEN, MLA_HEAD_PAD))
    sin_c = jnp.zeros((CTX_LEN, MLA_HEAD_PAD), jnp.float32)
    return jnp.concatenate([cos_c, cos_l], axis=0), jnp.concatenate([sin_c, sin_l], axis=0)


def _mla_prep(p32, cos, sin, gq, gkv, wq, wqr, wk, wv):
    B = p32.shape[0]
    full = lambda a: pl.BlockSpec(a.shape, lambda b, t: (0,) * a.ndim)
    rows = lambda width, blk: pl.BlockSpec((None, ROW_TILE, width), lambda b, t, blk=blk: (b, t, blk))
    tab = pl.BlockSpec((ROW_TILE, MLA_HEAD_PAD), lambda b, t: (t, 0))
    return pl.pallas_call(
        _mla_prep_body,
        grid=(B, N_ROW_TILES),
        in_specs=[rows(MLA_Q_RANK, P32_CQ // MLA_Q_RANK), rows(MLA_KV_RANK, P32_CKV // MLA_KV_RANK),
                  rows(LANES, P32_KR // LANES), rows(LANES, P32_KRP // LANES), tab, tab,
                  full(gq), full(gkv), full(wq), full(wqr), full(wk), full(wv)],
        out_specs=[rows(MLA_QK_WIDTH, 0), rows(MLA_QK_WIDTH, 0), rows(MLA_WIDTH, 0)],
        out_shape=[jax.ShapeDtypeStruct((B, TOK, MLA_QK_WIDTH), jnp.bfloat16),
                   jax.ShapeDtypeStruct((B, TOK, MLA_QK_WIDTH), jnp.bfloat16),
                   jax.ShapeDtypeStruct((B, TOK, MLA_WIDTH), jnp.bfloat16)],
        compiler_params=_cparams(2),
        name="mla_prep",
    )(p32, p32, p32, p32, cos, sin, gq, gkv, wq, wqr, wk, wv)


def _mla_attn_body(q_ref, k_ref, v_ref, o_ref, vm_s):
    lane = lax.broadcasted_iota(jnp.int32, (1, LANES), 1)
    v = v_ref[...]
    for hh in range(2):
        vm_s[hh] = jnp.where((lane >= MLA_V * hh) & (lane < MLA_V * (hh + 1)), v, jnp.zeros_like(v))

    def attend(qrows, n_keys):
        out = None
        for hh in range(2):
            sl = slice(hh * MLA_HEAD_PAD, (hh + 1) * MLA_HEAD_PAD)
            s = _dot_nt(q_ref[qrows, sl], k_ref[0:n_keys, sl])
            p = jnp.exp2(s - jnp.max(s, axis=-1, keepdims=True))
            den = jnp.sum(p, axis=-1, keepdims=True)
            o = _dot(p.astype(jnp.bfloat16), vm_s[hh, 0:n_keys, :]) / den
            out = o if out is None else out + o
        o_ref[qrows, :] = out.astype(o_ref.dtype)

    def loop_body(j, carry):
        attend(pl.ds(pl.multiple_of(CTX_LEN + j * MLA_TQ, MLA_TQ), MLA_TQ), TOK)
        return carry

    lax.fori_loop(0, SEQ // MLA_TQ, loop_body, 0)
    attend(slice(0, CTX_LEN), CTX_LEN)


def _mla_attention(q, k, v):
    B = q.shape[0]
    qk = pl.BlockSpec((None, TOK, 2 * MLA_HEAD_PAD), lambda b, hp: (b, 0, hp))
    vo = pl.BlockSpec((None, TOK, LANES), lambda b, hp: (b, 0, hp))
    return pl.pallas_call(
        _mla_attn_body,
        grid=(B, MLA_WIDTH // LANES),
        in_specs=[qk, qk, vo],
        out_specs=vo,
        out_shape=jax.ShapeDtypeStruct((B, TOK, MLA_WIDTH), jnp.bfloat16),
        scratch_shapes=[pltpu.VMEM((2, TOK, LANES), jnp.bfloat16)],
        compiler_params=_cparams(2),
        name="mla_attention",
    )(q, k, v)


def _out_proj_body(hg_ref, ml_ref, na_ref, x_ref, mod_ref, w_ref, o_ref):
    o = (_dot(hg_ref[...], w_ref[0:HG_WIDTH, :])
         + _dot(ml_ref[...], w_ref[HG_WIDTH:HG_WIDTH + MLA_WIDTH, :])
         + _dot(na_ref[...], w_ref[HG_WIDTH + MLA_WIDTH:, :]))
    o_ref[...] = x_ref[...] + mod_ref[2:3, :] * o


def _out_proj(hg, ml, na, xc, mods, w):
    B = xc.shape[0]
    rows = lambda width: pl.BlockSpec((None, ROW_TILE, width), lambda b, t: (b, t, 0))
    return pl.pallas_call(
        _out_proj_body,
        grid=(B, N_ROW_TILES),
        in_specs=[rows(HG_WIDTH), rows(MLA_WIDTH), rows(NA_WIDTH), rows(D_MODEL),
                  pl.BlockSpec((None, None, 6, D_MODEL), lambda b, t: (b, _kind_of_tile(t), 0, 0)),
                  pl.BlockSpec((D_MIX, D_MODEL), lambda b, t: (0, 0))],
        out_specs=rows(D_MODEL),
        out_shape=jax.ShapeDtypeStruct((B, TOK, D_MODEL), jnp.float32),
        compiler_params=_cparams(2),
        name="out_proj",
    )(hg, ml, na, xc, mods, w)


def _rmsnorm(x, g):
    x32 = x.astype(jnp.float32)
    y = x32 * lax.rsqrt(jnp.mean(x32 * x32, axis=-1, keepdims=True) + EPS)
    return (y * g.astype(jnp.float32)).astype(x.dtype)


def _ec_ffn(h, router_w, w1, w3, w2):
    B, L, D = h.shape
    cap = EC_CAPACITY * L // N_EXPERTS
    aff = jax.nn.softmax((h @ router_w).astype(jnp.float32), axis=-1)
    gate, idx = lax.top_k(aff.transpose(0, 2, 1), cap)
    xs = jax.vmap(lambda hb, ib: hb[ib])(h, idx)
    a = jnp.einsum('becd,edf->becf', xs, w1)
    u = jnp.einsum('becd,edf->becf', xs, w3)
    y = jnp.einsum('becf,efd->becd', jax.nn.silu(a) * u, w2) * gate[..., None].astype(h.dtype)
    return jax.vmap(lambda yb, ib: jnp.zeros((L, D), h.dtype).at[ib.reshape(-1)].add(yb.reshape(-1, D)))(y, idx)


def kernel(x, c, ctx, c_ctx, ada_w, ada_b, norm_mix_g, norm_ffn_g, w_in, hgrn_lb_logits,
           hgrn_onorm_g, mla_qnorm_g, mla_w_uq, mla_kvnorm_g, mla_w_ukv, na_rpb, w_out,
           router_w, exp_w1, exp_w3, exp_w2, final_norm_g):
    B = x.shape[0]
    lb_w = jax.nn.softmax(hgrn_lb_logits.astype(jnp.float32), axis=1)
    lb_all = jnp.cumsum(lb_w, axis=1) - lb_w[:, :1]
    s_lat = jax.nn.silu(c)
    s_ctx = jax.nn.silu(c_ctx)
    w_in_b = _widen_w_in(w_in)
    w_out_b = w_out.astype(jnp.bfloat16)
    wq, wqr, wk, wv = _mla_weights(mla_w_uq, mla_w_ukv)
    cos, sin = _rope_tables()
    xc = jnp.concatenate([ctx, x], axis=1)
    for layer in range(DEPTH):
        with_ctx_out = layer < DEPTH - 1
        mod_l = (s_lat @ ada_w[layer] + ada_b[layer]).reshape(B, 6, D_MODEL)
        mod_c = jnp.broadcast_to((s_ctx @ ada_w[layer] + ada_b[layer]).reshape(1, 6, D_MODEL), (B, 6, D_MODEL))
        mods = jnp.stack([mod_c, mod_l], axis=1)

        p32, p16 = _in_proj(xc, norm_mix_g[layer].reshape(1, D_MODEL), mods, w_in_b[layer])
        hg = _hgrn_mixer(p32, lb_all[:, layer], hgrn_onorm_g[layer])
        q, k, v = _mla_prep(p32, cos, sin, mla_qnorm_g[layer].reshape(1, -1), mla_kvnorm_g[layer].reshape(1, -1),
                            wq[layer], wqr[layer], wk[layer], wv[layer])
        ml = _mla_attention(q, k, v)
        na = _na_attention(p16, _na_bias_tables(na_rpb[layer]))
        xc = _out_proj(hg, ml, na, xc, mods, w_out_b[layer])

        x, ctx = xc[:, CTX_LEN:], xc[:, :CTX_LEN]
        sh_f, sc_f, g_f = mod_l[:, 3:4], mod_l[:, 4:5], mod_l[:, 5:6]
        h = _rmsnorm(x, norm_ffn_g[layer]) * (1.0 + sc_f) + sh_f
        x = x + g_f * _ec_ffn(h, router_w[layer], exp_w1[layer], exp_w3[layer], exp_w2[layer])
        if with_ctx_out:
            csh_f, csc_f, cg_f = mod_c[:, 3:4], mod_c[:, 4:5], mod_c[:, 5:6]
            hc = _rmsnorm(ctx, norm_ffn_g[layer]) * (1.0 + csc_f) + csh_f
            ctx = ctx + cg_f * _ec_ffn(hc, router_w[layer], exp_w1[layer], exp_w3[layer], exp_w2[layer])
        xc = jnp.concatenate([ctx, x], axis=1)
    return _rmsnorm(xc[:, CTX_LEN:], final_norm_g)
```

```python
import functools
import math

import numpy as np
import jax
import jax.numpy as jnp
from jax import lax
from jax.experimental import pallas as pl
from jax.experimental.pallas import tpu as pltpu

D_MODEL = 1024
BATCH = 8
SEQ = 4096
DEPTH = 4

CTX_LEN = 256
GRID_W = 64
EPS = 1e-6
MIN_FORGET = 1e-20

D_MIX = D_MODEL
HG_HEADS = 4
HG_DIM = 64
HG_WIDTH = HG_HEADS * HG_DIM
HG_CHUNK = 64
MLA_HEADS = 6
MLA_NOPE = 64
MLA_ROPE = 32
MLA_V = 64
MLA_Q_RANK = 256
MLA_KV_RANK = 128
MLA_WIDTH = MLA_HEADS * MLA_V
NA_HEADS = 6
NA_DIM = 64
NA_WIDTH = NA_HEADS * NA_DIM
NA_WIN_R = 8
NA_WIN_C = 16
HG_IN = 5 * HG_WIDTH
MLA_IN = MLA_Q_RANK + MLA_KV_RANK + MLA_ROPE
NA_IN = 3 * NA_WIDTH
D_IN = HG_IN + MLA_IN + NA_IN
ROPE_BASE = 10000.0
N_EXPERTS = 16
EC_CAPACITY = 2
F_EXPERT = 1024

LANES = 128
SUBLANES = 8
VMEM_LIMIT_BYTES = 56 * 1024 * 1024

TOK = CTX_LEN + SEQ
ROW_TILE = 256
N_ROW_TILES = TOK // ROW_TILE
CTX_TILES = CTX_LEN // ROW_TILE

MLA_HEAD_PAD = LANES
P32_HG = 0
P32_CQ = HG_IN
P32_CKV = P32_CQ + MLA_Q_RANK
P32_KR = P32_CKV + MLA_KV_RANK
P32_KRP = P32_KR + LANES
P32_WIDTH = P32_KRP + LANES
P_WIDTH = P32_WIDTH + NA_IN


def _cparams(n_grid_dims):
    return pltpu.CompilerParams(dimension_semantics=("arbitrary",) * n_grid_dims,
                                vmem_limit_bytes=VMEM_LIMIT_BYTES)


def _dot_nt(a, b):
    return lax.dot_general(a, b, (((1,), (1,)), ((), ())), preferred_element_type=jnp.float32)


def _dot(a, b):
    return jnp.dot(a, b, preferred_element_type=jnp.float32)


def _kind_of_tile(t):
    return (t >= CTX_TILES).astype(jnp.int32)


MOD_SHIFT_A, MOD_SCALE_A, MOD_GATE_A, MOD_SHIFT_F, MOD_SCALE_F, MOD_GATE_F = range(6)


def _mod_row(mod_ref, i):
    return mod_ref[i:i + 1, :]


def _modulated_norm(x, g, shift, scale):
    y = x * lax.rsqrt(jnp.mean(x * x, axis=-1, keepdims=True) + EPS) * g
    return y * (1.0 + scale) + shift


def _in_proj_body(*refs, has_delta):
    if has_delta:
        x_ref, delta_ref, pmod_ref, g_ref, mod_ref, w_ref, xo_ref, p32_ref, p16_ref = refs
        x = x_ref[...] + _mod_row(pmod_ref, MOD_GATE_F) * delta_ref[...]
        xo_ref[...] = x
    else:
        x_ref, g_ref, mod_ref, w_ref, p32_ref, p16_ref = refs
        x = x_ref[...]
    h = _modulated_norm(x, g_ref[...], _mod_row(mod_ref, MOD_SHIFT_A), _mod_row(mod_ref, MOD_SCALE_A))
    p = _dot(h.astype(jnp.bfloat16), w_ref[...])
    p32_ref[...] = p[:, :P32_WIDTH]
    p16_ref[...] = p[:, P32_WIDTH:].astype(p16_ref.dtype)


def _row_spec(width):
    return pl.BlockSpec((None, ROW_TILE, width), lambda b, t: (b, t, 0))


def _mod_spec():
    return pl.BlockSpec((None, None, 6, D_MODEL), lambda b, t: (b, _kind_of_tile(t), 0, 0))


def _in_proj(xc, delta, prev_mods, g, mods, w):
    B = xc.shape[0]
    has_delta = delta is not None
    fixed = [pl.BlockSpec((1, D_MODEL), lambda b, t: (0, 0)), _mod_spec(),
             pl.BlockSpec((D_MODEL, P_WIDTH), lambda b, t: (0, 0))]
    outs = [_row_spec(P32_WIDTH), _row_spec(NA_IN)]
    shapes = [jax.ShapeDtypeStruct((B, TOK, P32_WIDTH), jnp.float32),
              jax.ShapeDtypeStruct((B, TOK, NA_IN), jnp.bfloat16)]
    if has_delta:
        in_specs = [_row_spec(D_MODEL), _row_spec(D_MODEL), _mod_spec()] + fixed
        args = (xc, delta, prev_mods, g, mods, w)
        outs = [_row_spec(D_MODEL)] + outs
        shapes = [jax.ShapeDtypeStruct((B, TOK, D_MODEL), jnp.float32)] + shapes
    else:
        in_specs = [_row_spec(D_MODEL)] + fixed
        args = (xc, g, mods, w)
    res = pl.pallas_call(
        functools.partial(_in_proj_body, has_delta=has_delta),
        grid=(B, N_ROW_TILES),
        in_specs=in_specs,
        out_specs=outs,
        out_shape=shapes,
        compiler_params=_cparams(2),
        name="in_proj",
    )(*args)
    return res if has_delta else [xc] + list(res)


def _rope_partner():
    half = MLA_ROPE // 2
    quarter = half // 2
    partner, sign = [], []
    for r in range(MLA_ROPE):
        base, j = (r // half) * half, r % half
        partner.append(base + (j + quarter if j < quarter else j - quarter))
        sign.append(-1.0 if j < quarter else 1.0)
    return np.array(partner), np.array(sign, np.float32)


def _widen_w_in(w_in):
    L = w_in.shape[0]
    a0, a1 = HG_IN, HG_IN + MLA_IN
    partner, sign = _rope_partner()
    kr = w_in[..., a0 + MLA_Q_RANK + MLA_KV_RANK:a1]
    zl = jnp.zeros((L, D_MODEL, MLA_NOPE), w_in.dtype)
    zr = jnp.zeros((L, D_MODEL, MLA_HEAD_PAD - MLA_NOPE - MLA_ROPE), w_in.dtype)
    na = w_in[..., a1:]
    return jnp.concatenate([
        w_in[..., :a0 + MLA_Q_RANK + MLA_KV_RANK],
        zl, kr, zr,
        zl, kr[..., partner] * sign, zr,
        na[..., :NA_WIDTH] * NA_DIM ** -0.5, na[..., NA_WIDTH:]], axis=-1).astype(jnp.bfloat16)


GRID_H = SEQ // GRID_W
NA_QROWS = 2
NA_QB = NA_QROWS * GRID_W
NA_KROWS = NA_QROWS + NA_WIN_R - 1
NA_KB = NA_KROWS * GRID_W
NA_NBLK = GRID_H // NA_QROWS
NA_MASKED = -1e30


def _na_window_start(m):
    return min(max(NA_QROWS * m - NA_WIN_R // 2, 0), GRID_H - NA_KROWS)


@functools.lru_cache(maxsize=None)
def _na_patterns():
    qi = np.arange(NA_QB)
    kj = np.arange(NA_KB)
    c = (qi % GRID_W)[:, None]
    kc = (kj % GRID_W)[None, :]
    cs = np.clip(c - NA_WIN_C // 2, 0, GRID_W - NA_WIN_C)
    tables, pat_of_block = [], []
    for m in range(NA_NBLK):
        r = (NA_QROWS * m + qi // GRID_W)[:, None]
        kr = (_na_window_start(m) + kj // GRID_W)[None, :]
        rs = np.clip(r - NA_WIN_R // 2, 0, GRID_H - NA_WIN_R)
        valid = (kr >= rs) & (kr < rs + NA_WIN_R) & (kc >= cs) & (kc < cs + NA_WIN_C)
        idx = np.where(valid, (kr - r + NA_WIN_R - 1) * (2 * NA_WIN_C - 1) + (kc - c + NA_WIN_C - 1), 0)
        key = (idx.tobytes(), valid.tobytes())
        for p, (k2, _, _) in enumerate(tables):
            if k2 == key:
                pat_of_block.append(p)
                break
        else:
            pat_of_block.append(len(tables))
            tables.append((key, idx, valid))
    idx = np.stack([t[1] for t in tables]).astype(np.int32)
    valid = np.stack([t[2] for t in tables])
    return idx, valid, tuple(pat_of_block)


def _na_bias_tables(rpb):
    idx, valid, _ = _na_patterns()
    flat = rpb.reshape(NA_HEADS, -1).astype(jnp.float32)
    return jnp.where(valid[None], flat[:, idx], NA_MASKED)


def _na_body(q_ref, k_ref, v_ref, bias_ref, o_ref):
    _, _, pat_of_block = _na_patterns()
    lane = lax.broadcasted_iota(jnp.int32, (1, LANES), 1)
    head_masks = [(lane >= NA_DIM * hh) & (lane < NA_DIM * (hh + 1)) for hh in range(2)]
    kc = k_ref[0:CTX_LEN, :]
    vc = v_ref[0:CTX_LEN, :]
    vc_h = [jnp.where(hm, vc, jnp.zeros_like(vc)) for hm in head_masks]

    def block(m, ws, pat):
        qrows = pl.ds(pl.multiple_of(CTX_LEN + m * NA_QB, NA_QB), NA_QB)
        q = q_ref[qrows, :]
        krows = pl.ds(pl.multiple_of(CTX_LEN + ws * GRID_W, GRID_W), NA_KB)
        kwin = k_ref[krows, :]
        vwin = v_ref[krows, :]
        out = None
        for hh in range(2):
            qh = jnp.where(head_masks[hh], q, jnp.zeros_like(q))
            s_w = _dot_nt(qh, kwin) + bias_ref[hh, pat]
            s_c = _dot_nt(qh, kc)
            mx = jnp.maximum(jnp.max(s_w, axis=-1, keepdims=True), jnp.max(s_c, axis=-1, keepdims=True))
            p_w = jnp.exp(s_w - mx)
            p_c = jnp.exp(s_c - mx)
            den = jnp.sum(p_w, axis=-1, keepdims=True) + jnp.sum(p_c, axis=-1, keepdims=True)
            vh = jnp.where(head_masks[hh], vwin, jnp.zeros_like(vwin))
            o = _dot(p_w.astype(jnp.bfloat16), vh) + _dot(p_c.astype(jnp.bfloat16), vc_h[hh])
            o = o / den
            out = o if out is None else out + o
        o_ref[qrows, :] = out.astype(o_ref.dtype)

    interior = [m for m in range(NA_NBLK) if pat_of_block.count(pat_of_block[m]) > 1]
    lo, hi = interior[0], interior[-1] + 1
    assert interior == list(range(lo, hi)) and len({pat_of_block[m] for m in interior}) == 1
    for m in list(range(lo)) + list(range(hi, NA_NBLK)):
        block(m, _na_window_start(m), pat_of_block[m])

    def loop_body(m, carry):
        block(m, NA_QROWS * m - NA_WIN_R // 2, pat_of_block[lo])
        return carry

    lax.fori_loop(lo, hi, loop_body, 0)

    qc = q_ref[0:CTX_LEN, :]
    out = None
    for hh in range(2):
        qh = jnp.where(head_masks[hh], qc, jnp.zeros_like(qc))
        s = _dot_nt(qh, kc)
        p = jnp.exp(s - jnp.max(s, axis=-1, keepdims=True))
        o = _dot(p.astype(jnp.bfloat16), vc_h[hh]) / jnp.sum(p, axis=-1, keepdims=True)
        out = o if out is None else out + o
    o_ref[0:CTX_LEN, :] = out.astype(o_ref.dtype)


def _na_attention(p16, bias):
    B = p16.shape[0]
    n_pat = bias.shape[1]
    n_pairs = NA_WIDTH // LANES

    def col(group):
        return pl.BlockSpec((None, TOK, LANES), lambda b, hp, group=group: (b, 0, group * n_pairs + hp))

    return pl.pallas_call(
        _na_body,
        grid=(B, n_pairs),
        in_specs=[col(0), col(1), col(2),
                  pl.BlockSpec((2, n_pat, NA_QB, NA_KB), lambda b, hp: (hp, 0, 0, 0))],
        out_specs=col(0),
        out_shape=jax.ShapeDtypeStruct((B, TOK, NA_WIDTH), jnp.bfloat16),
        compiler_params=_cparams(2),
        name="na_attention",
    )(p16, p16, p16, bias)


HG_NCHUNK = TOK // HG_CHUNK
HG_CTX_CHUNKS = CTX_LEN // HG_CHUNK


def _split3_bf16(x):
    hi = x.astype(jnp.bfloat16)
    r1 = x - hi.astype(jnp.float32)
    mid = r1.astype(jnp.bfloat16)
    lo = (r1 - mid.astype(jnp.float32)).astype(jnp.bfloat16)
    return hi, mid, lo


def _hgrn_body(q_ref, i_ref, zf_ref, zb_ref, g_ref, lb_ref, gn_ref, o_ref,
               of_s, b_s, k_s, q_s, v_s, oi_s):
    C = HG_CHUNK
    r_i = lax.broadcasted_iota(jnp.int32, (C, C), 0)
    c_i = lax.broadcasted_iota(jnp.int32, (C, C), 1)
    tri = {True: (c_i <= r_i).astype(jnp.bfloat16), False: (c_i >= r_i).astype(jnp.bfloat16)}
    r2 = lax.broadcasted_iota(jnp.int32, (LANES, LANES), 0) // HG_DIM
    c2 = lax.broadcasted_iota(jnp.int32, (LANES, LANES), 1) // HG_DIM
    same_head = r2 == c2
    half_ones = same_head.astype(jnp.bfloat16)
    row_id = lax.broadcasted_iota(jnp.int32, (C, LANES), 0)
    lane_lo = lax.broadcasted_iota(jnp.int32, (1, LANES), 1) < HG_DIM

    def chunk(c_idx, s_t, fwd):
        rows = pl.ds(pl.multiple_of(c_idx * C, C), C)
        lb = lb_ref[0:1, :] if fwd else lb_ref[1:2, :]
        z = (zf_ref if fwd else zb_ref)[rows, :]
        f = jnp.maximum(lb + (1.0 - lb) * jax.nn.sigmoid(z), MIN_FORGET)
        lf = jnp.log(f)
        hi, mid, lo = _split3_bf16(lf)
        b = _dot(tri[fwd], hi) + _dot(tri[fwd], mid) + _dot(tri[fwd], lo)
        q = q_ref[rows, :] * (HG_DIM ** -0.5)
        v = i_ref[rows, :]
        b_s[...] = b
        k_s[...] = 1.0 - f
        q_s[...] = q
        v_s[...] = v
        b_last = b[C - 1:C, :] if fwd else b[0:1, :]

        for i in range(C):
            g0, g1 = (0, i // SUBLANES + 1) if fwd else (i // SUBLANES, C // SUBLANES)
            blk = slice(g0 * SUBLANES, g1 * SUBLANES)
            e = jnp.exp(jnp.minimum(b_s[i:i + 1, :] - b_s[blk, :], 0.0))
            cmat = e * k_s[blk, :] * q_s[i:i + 1, :]
            visible = (row_id[blk, :] <= i) if fwd else (row_id[blk, :] >= i)
            cmat = jnp.where(visible, cmat, 0.0)
            a_b = _dot(cmat.astype(jnp.bfloat16), half_ones)
            oi_s[i:i + 1, :] = jnp.sum(a_b * v_s[blk, :], axis=0, keepdims=True)

        qe = (q * jnp.exp(b)).astype(jnp.bfloat16)
        o = oi_s[...] + _dot_nt(qe, s_t.astype(jnp.bfloat16))
        ke = (k_s[...] * jnp.exp(b_last - b)).astype(jnp.bfloat16)
        upd = _dot(v.T.astype(jnp.bfloat16), ke)
        s_t = s_t * jnp.exp(b_last) + jnp.where(same_head, upd, 0.0)
        return rows, o, s_t

    def fwd_body(n, s_t):
        rows, o, s_t = chunk(n, s_t, True)
        of_s[rows, :] = o
        return s_t

    def bwd_body(n, s_t):
        c_idx = jnp.where(n < HG_CTX_CHUNKS, HG_CTX_CHUNKS - 1 - n, HG_NCHUNK + HG_CTX_CHUNKS - 1 - n)
        rows, o, s_t = chunk(c_idx, s_t, False)
        o = o + of_s[rows, :]
        sq = o * o
        ms = jnp.where(lane_lo,
                       jnp.sum(jnp.where(lane_lo, sq, 0.0), axis=-1, keepdims=True),
                       jnp.sum(jnp.where(lane_lo, 0.0, sq), axis=-1, keepdims=True)) * (1.0 / HG_DIM)
        y = o * lax.rsqrt(ms + EPS) * gn_ref[...]
        g = g_ref[rows, :]
        o_ref[rows, :] = (y * (g * jax.nn.sigmoid(g))).astype(o_ref.dtype)
        return s_t

    zero = jnp.zeros((LANES, LANES), jnp.float32)
    lax.fori_loop(0, HG_NCHUNK, fwd_body, zero)
    lax.fori_loop(0, HG_NCHUNK, bwd_body, zero)


def _hgrn_mixer(p32, lb, onorm_g):
    B = p32.shape[0]
    n_pairs = HG_WIDTH // LANES

    def col(group):
        return pl.BlockSpec((None, TOK, LANES), lambda b, hp, group=group: (b, 0, group * n_pairs + hp))

    gn = jnp.tile(onorm_g.astype(jnp.float32), LANES // HG_DIM).reshape(1, LANES)
    chunk_scratch = pltpu.VMEM((HG_CHUNK, LANES), jnp.float32)
    return pl.pallas_call(
        _hgrn_body,
        grid=(B, n_pairs),
        in_specs=[col(0), col(1), col(2), col(3), col(4),
                  pl.BlockSpec((2, LANES), lambda b, hp: (0, hp)),
                  pl.BlockSpec((1, LANES), lambda b, hp: (0, 0))],
        out_specs=pl.BlockSpec((None, TOK, LANES), lambda b, hp: (b, 0, hp)),
        out_shape=jax.ShapeDtypeStruct((B, TOK, HG_WIDTH), jnp.bfloat16),
        scratch_shapes=[pltpu.VMEM((TOK, LANES), jnp.float32)] + [chunk_scratch] * 5,
        compiler_params=_cparams(2),
        name="hgrn2",
    )(p32, p32, p32, p32, p32, lb.astype(jnp.float32), gn)


MLA_QK_WIDTH = MLA_HEADS * MLA_HEAD_PAD
MLA_QSCALE = (MLA_NOPE + MLA_ROPE) ** -0.5 * math.log2(math.e)
MLA_TQ = 256


def _mla_prep_body(cq_ref, ckv_ref, kr_ref, krp_ref, cos_ref, sin_ref, gq_ref, gkv_ref,
                   wq_ref, wqr_ref, wk_ref, wv_ref, q_out, k_out, v_out):
    def rms(x, g):
        return (x * lax.rsqrt(jnp.mean(x * x, axis=-1, keepdims=True) + EPS) * g).astype(jnp.bfloat16)

    cos = cos_ref[...]
    sin = sin_ref[...]
    cqn = rms(cq_ref[...], gq_ref[...])
    q = _dot(cqn, wq_ref[...])
    qr = _dot(cqn, wqr_ref[...])
    ckvn = rms(ckv_ref[...], gkv_ref[...])
    kn = _dot(ckvn, wk_ref[...])
    v_out[...] = _dot(ckvn, wv_ref[...]).astype(v_out.dtype)
    k_rope = kr_ref[...] * cos + krp_ref[...] * sin
    for h in range(MLA_HEADS):
        sl = slice(h * MLA_HEAD_PAD, (h + 1) * MLA_HEAD_PAD)
        q_out[:, sl] = ((q[:, sl] * cos + qr[:, sl] * sin) * MLA_QSCALE).astype(q_out.dtype)
        k_out[:, sl] = (kn[:, sl] + k_rope).astype(k_out.dtype)


def _mla_weights(w_uq, w_ukv):
    partner, sign = _rope_partner()
    L = w_uq.shape[0]
    wq = w_uq.reshape(L, MLA_Q_RANK, MLA_HEADS, MLA_NOPE + MLA_ROPE)
    q_nope, q_rope = wq[..., :MLA_NOPE], wq[..., MLA_NOPE:]
    pad = jnp.zeros(wq.shape[:3] + (MLA_HEAD_PAD - MLA_NOPE - MLA_ROPE,), w_uq.dtype)
    wq_pad = jnp.concatenate([q_nope, q_rope, pad], axis=-1)
    wq_rot = jnp.concatenate([jnp.zeros_like(q_nope), q_rope[..., partner] * sign, pad], axis=-1)
    wkv = w_ukv.reshape(L, MLA_KV_RANK, MLA_HEADS, MLA_NOPE + MLA_V)
    k_nope, v = wkv[..., :MLA_NOPE], wkv[..., MLA_NOPE:]
    wk_pad = jnp.concatenate([k_nope, jnp.zeros(k_nope.shape[:3] + (MLA_HEAD_PAD - MLA_NOPE,), w_ukv.dtype)], axis=-1)
    bf = lambda a, n: a.reshape(L, a.shape[1], n).astype(jnp.bfloat16)
    return bf(wq_pad, MLA_QK_WIDTH), bf(wq_rot, MLA_QK_WIDTH), bf(wk_pad, MLA_QK_WIDTH), bf(v, MLA_WIDTH)


def _rope_tables():
    t = jnp.arange(SEQ)
    half = MLA_ROPE // 2
    inv = ROPE_BASE ** (-jnp.arange(0, half, 2, dtype=jnp.float32) / half)
    ang_r = (t // GRID_W).astype(jnp.float32)[:, None] * inv
    ang_c = (t % GRID_W).astype(jnp.float32)[:, None] * inv
    ang = jnp.concatenate([ang_r, ang_r, ang_c, ang_c], axis=-1)
    ones = jnp.ones((SEQ, MLA_NOPE), jnp.float32)
    zeros_n = jnp.zeros((SEQ, MLA_NOPE), jnp.float32)
    zeros_p = jnp.zeros((SEQ, MLA_HEAD_PAD - MLA_NOPE - MLA_ROPE), jnp.float32)
    cos_l = jnp.concatenate([ones, jnp.cos(ang), zeros_p], axis=-1)
    sin_l = jnp.concatenate([zeros_n, jnp.sin(ang), zeros_p], axis=-1)
    lane = np.arange(MLA_HEAD_PAD)
    cos_c = jnp.broadcast_to(jnp.asarray((lane < MLA_NOPE + MLA_ROPE).astype(np.float32)), (CTX_LEN, MLA_HEAD_PAD))
    sin_c = jnp.zeros((CTX_LEN, MLA_HEAD_PAD), jnp.float32)
    return jnp.concatenate([cos_c, cos_l], axis=0), jnp.concatenate([sin_c, sin_l], axis=0)


def _mla_prep(p32, cos, sin, gq, gkv, wq, wqr, wk, wv):
    B = p32.shape[0]
    full = lambda a: pl.BlockSpec(a.shape, lambda b, t: (0,) * a.ndim)
    rows = lambda width, blk: pl.BlockSpec((None, ROW_TILE, width), lambda b, t, blk=blk: (b, t, blk))
    tab = pl.BlockSpec((ROW_TILE, MLA_HEAD_PAD), lambda b, t: (t, 0))
    return pl.pallas_call(
        _mla_prep_body,
        grid=(B, N_ROW_TILES),
        in_specs=[rows(MLA_Q_RANK, P32_CQ // MLA_Q_RANK), rows(MLA_KV_RANK, P32_CKV // MLA_KV_RANK),
                  rows(LANES, P32_KR // LANES), rows(LANES, P32_KRP // LANES), tab, tab,
                  full(gq), full(gkv), full(wq), full(wqr), full(wk), full(wv)],
        out_specs=[rows(MLA_QK_WIDTH, 0), rows(MLA_QK_WIDTH, 0), rows(MLA_WIDTH, 0)],
        out_shape=[jax.ShapeDtypeStruct((B, TOK, MLA_QK_WIDTH), jnp.bfloat16),
                   jax.ShapeDtypeStruct((B, TOK, MLA_QK_WIDTH), jnp.bfloat16),
                   jax.ShapeDtypeStruct((B, TOK, MLA_WIDTH), jnp.bfloat16)],
        compiler_params=_cparams(2),
        name="mla_prep",
    )(p32, p32, p32, p32, cos, sin, gq, gkv, wq, wqr, wk, wv)


def _mla_attn_body(q_ref, k_ref, v_ref, o_ref, vm_s):
    lane = lax.broadcasted_iota(jnp.int32, (1, LANES), 1)
    v = v_ref[...]
    for hh in range(2):
        vm_s[hh] = jnp.where((lane >= MLA_V * hh) & (lane < MLA_V * (hh + 1)), v, jnp.zeros_like(v))

    def attend(qrows, n_keys):
        out = None
        for hh in range(2):
            sl = slice(hh * MLA_HEAD_PAD, (hh + 1) * MLA_HEAD_PAD)
            s = _dot_nt(q_ref[qrows, sl], k_ref[0:n_keys, sl])
            p = jnp.exp2(s - jnp.max(s, axis=-1, keepdims=True))
            den = jnp.sum(p, axis=-1, keepdims=True)
            o = _dot(p.astype(jnp.bfloat16), vm_s[hh, 0:n_keys, :]) / den
            out = o if out is None else out + o
        o_ref[qrows, :] = out.astype(o_ref.dtype)

    def loop_body(j, carry):
        attend(pl.ds(pl.multiple_of(CTX_LEN + j * MLA_TQ, MLA_TQ), MLA_TQ), TOK)
        return carry

    lax.fori_loop(0, SEQ // MLA_TQ, loop_body, 0)
    attend(slice(0, CTX_LEN), CTX_LEN)


def _mla_attention(q, k, v):
    B = q.shape[0]
    qk = pl.BlockSpec((None, TOK, 2 * MLA_HEAD_PAD), lambda b, hp: (b, 0, hp))
    vo = pl.BlockSpec((None, TOK, LANES), lambda b, hp: (b, 0, hp))
    return pl.pallas_call(
        _mla_attn_body,
        grid=(B, MLA_WIDTH // LANES),
        in_specs=[qk, qk, vo],
        out_specs=vo,
        out_shape=jax.ShapeDtypeStruct((B, TOK, MLA_WIDTH), jnp.bfloat16),
        scratch_shapes=[pltpu.VMEM((2, TOK, LANES), jnp.bfloat16)],
        compiler_params=_cparams(2),
        name="mla_attention",
    )(q, k, v)


HALF_D = D_MODEL // 2
HI16 = 0xFFFF0000


def _out_proj_body(hg_ref, ml_ref, na_ref, x_ref, mod_ref, w_ref, gf_ref, wr_ref, o_ref, hp_ref, aff_ref):
    o = (_dot(hg_ref[...], w_ref[0:HG_WIDTH, :])
         + _dot(ml_ref[...], w_ref[HG_WIDTH:HG_WIDTH + MLA_WIDTH, :])
         + _dot(na_ref[...], w_ref[HG_WIDTH + MLA_WIDTH:, :]))
    x = x_ref[...] + _mod_row(mod_ref, MOD_GATE_A) * o
    o_ref[...] = x
    h = _modulated_norm(x, gf_ref[...], _mod_row(mod_ref, MOD_SHIFT_F), _mod_row(mod_ref, MOD_SCALE_F))
    hb = h.astype(jnp.bfloat16)
    bits = pltpu.bitcast(hb.astype(jnp.float32), jnp.uint32)
    hp_ref[...] = (bits[:, HALF_D:] & jnp.uint32(HI16)) | (bits[:, :HALF_D] >> 16)
    logits = _dot(hb, wr_ref[...])
    lane = lax.broadcasted_iota(jnp.int32, logits.shape, 1)
    logits = jnp.where(lane < N_EXPERTS, logits, NA_MASKED)
    e = jnp.exp(logits - jnp.max(logits, axis=-1, keepdims=True))
    aff = e / jnp.sum(e, axis=-1, keepdims=True)
    aff_ref[...] = aff.T[0:N_EXPERTS, :]


def _out_proj(hg, ml, na, xc, mods, w, g_ffn, w_router):
    B = xc.shape[0]
    return pl.pallas_call(
        _out_proj_body,
        grid=(B, N_ROW_TILES),
        in_specs=[_row_spec(HG_WIDTH), _row_spec(MLA_WIDTH), _row_spec(NA_WIDTH), _row_spec(D_MODEL),
                  _mod_spec(),
                  pl.BlockSpec((D_MIX, D_MODEL), lambda b, t: (0, 0)),
                  pl.BlockSpec((1, D_MODEL), lambda b, t: (0, 0)),
                  pl.BlockSpec((D_MODEL, LANES), lambda b, t: (0, 0))],
        out_specs=[_row_spec(D_MODEL), _row_spec(HALF_D),
                   pl.BlockSpec((None, N_EXPERTS, ROW_TILE), lambda b, t: (b, 0, t))],
        out_shape=[jax.ShapeDtypeStruct((B, TOK, D_MODEL), jnp.float32),
                   jax.ShapeDtypeStruct((B, TOK, HALF_D), jnp.uint32),
                   jax.ShapeDtypeStruct((B, N_EXPERTS, TOK), jnp.float32)],
        compiler_params=_cparams(2),
        name="out_proj",
    )(hg, ml, na, xc, mods, w, g_ffn, w_router)


CAP_CTX = EC_CAPACITY * CTX_LEN // N_EXPERTS
CAP_LAT = EC_CAPACITY * SEQ // N_EXPERTS
CAP = CAP_CTX + CAP_LAT
IDX_PAD = -(-CAP // LANES) * LANES
ONE_BITS = 0x3F800000


def _prefix_count(m, upper):
    outs, off = [], jnp.zeros((m.shape[0], 1), jnp.float32)
    for j in range(m.shape[1] // LANES):
        loc = _dot(m[:, j * LANES:(j + 1) * LANES].astype(jnp.bfloat16), upper)
        outs.append(loc + off)
        off = off + loc[:, LANES - 1:LANES]
    return jnp.concatenate(outs, axis=1)


def _select_top(bits, k, upper):
    rows = bits.shape[0]

    def step(_, lohi):
        lo, hi = lohi
        mid = lo + ((hi - lo + 1) >> 1)
        cnt = jnp.sum((bits >= mid).astype(jnp.int32), axis=-1, keepdims=True)
        ok = cnt >= k
        return jnp.where(ok, mid, lo), jnp.where(ok, hi, mid - 1)

    thr, _ = lax.fori_loop(0, 31, step, (jnp.zeros((rows, 1), jnp.int32), jnp.full((rows, 1), ONE_BITS, jnp.int32)))
    gt = (bits > thr).astype(jnp.float32)
    eq = (bits == thr).astype(jnp.float32)
    need = k - jnp.sum(gt, axis=-1, keepdims=True)
    return gt + eq * (_prefix_count(eq, upper) <= need).astype(jnp.float32)


def _route_body(aff_ref, idx_ref):
    bits = pltpu.bitcast(aff_ref[...], jnp.int32)
    r = lax.broadcasted_iota(jnp.int32, (LANES, LANES), 0)
    c = lax.broadcasted_iota(jnp.int32, (LANES, LANES), 1)
    upper = (r <= c).astype(jnp.bfloat16)
    sel = jnp.concatenate([_select_top(bits[:, :CTX_LEN], CAP_CTX, upper),
                           _select_top(bits[:, CTX_LEN:], CAP_LAT, upper)], axis=1)
    pos = _prefix_count(sel, upper)
    lane = lax.broadcasted_iota(jnp.int32, (LANES, LANES), 1)
    for sb in range(IDX_PAD // LANES):
        slot = (lax.broadcasted_iota(jnp.int32, (LANES, 1), 0) + sb * LANES).astype(jnp.float32)
        cols = jnp.zeros((LANES, LANES), jnp.float32)
        for e in range(N_EXPERTS):
            cnt = jnp.sum((pos[e:e + 1, :] <= slot).astype(jnp.float32), axis=-1, keepdims=True)
            cols = jnp.where(lane == e, cnt, cols)
        idx = jnp.minimum(cols.T[0:N_EXPERTS, :], TOK - 1)
        idx_ref[:, sb * LANES:(sb + 1) * LANES] = idx.astype(jnp.int32)


def _route(aff):
    B = aff.shape[0]
    return pl.pallas_call(
        _route_body,
        grid=(B,),
        in_specs=[pl.BlockSpec((None, N_EXPERTS, TOK), lambda b: (b, 0, 0))],
        out_specs=pl.BlockSpec((None, N_EXPERTS, IDX_PAD), lambda b: (b, 0, 0)),
        out_shape=jax.ShapeDtypeStruct((B, N_EXPERTS, IDX_PAD), jnp.int32),
        compiler_params=_cparams(1),
        name="route",
    )(aff)


ROW_UNROLL = 8


def _experts_body(idx_ref, aff_ref, hp_hbm, w1_ref, w3_ref, w2_ref, delta_hbm, hp_s, acc_s, xs_s, y_s, sem):
    b = pl.program_id(0)
    e = pl.program_id(1)

    @pl.when(e == 0)
    def _():
        load = pltpu.make_async_copy(hp_hbm.at[b], hp_s, sem.at[0])
        load.start()
        acc_s[...] = jnp.zeros_like(acc_s)
        load.wait()

    def gather(s, carry):
        xs_s[pl.ds(s, 1), :] = hp_s[pl.ds(idx_ref[0, s], 1), :]
        return carry

    lax.fori_loop(0, CAP, gather, 0, unroll=ROW_UNROLL)

    xs = xs_s[...]
    lo = pltpu.bitcast(xs << 16, jnp.float32).astype(jnp.bfloat16)
    hi = pltpu.bitcast(xs & jnp.uint32(HI16), jnp.float32).astype(jnp.bfloat16)
    a = _dot(lo, w1_ref[0:HALF_D, :]) + _dot(hi, w1_ref[HALF_D:, :])
    u = _dot(lo, w3_ref[0:HALF_D, :]) + _dot(hi, w3_ref[HALF_D:, :])
    mid = (a * jax.nn.sigmoid(a) * u).astype(jnp.bfloat16)
    y_s[...] = _dot(mid, w2_ref[...])

    def scatter(s, carry):
        row = idx_ref[0, s]
        acc_s[pl.ds(row, 1), :] += aff_ref[0, row] * y_s[pl.ds(s, 1), :]
        return carry

    lax.fori_loop(0, CAP, scatter, 0, unroll=ROW_UNROLL)

    @pl.when(e == N_EXPERTS - 1)
    def _():
        store = pltpu.make_async_copy(acc_s, delta_hbm.at[b], sem.at[1])
        store.start()
        store.wait()


def _experts(idx, aff, hp, w1, w3, w2):
    B = hp.shape[0]
    smem = lambda n: pl.BlockSpec((None, None, 1, n), lambda b, e: (b, e, 0, 0), memory_space=pltpu.SMEM)
    wspec = lambda k, n: pl.BlockSpec((None, k, n), lambda b, e: (e, 0, 0))
    return pl.pallas_call(
        _experts_body,
        grid=(B, N_EXPERTS),
        in_specs=[smem(IDX_PAD), smem(TOK), pl.BlockSpec(memory_space=pl.ANY),
                  wspec(D_MODEL, F_EXPERT), wspec(D_MODEL, F_EXPERT), wspec(F_EXPERT, D_MODEL)],
        out_specs=pl.BlockSpec(memory_space=pl.ANY),
        out_shape=jax.ShapeDtypeStruct((B, TOK, D_MODEL), jnp.float32),
        scratch_shapes=[pltpu.VMEM((TOK, HALF_D), jnp.uint32), pltpu.VMEM((TOK, D_MODEL), jnp.float32),
                        pltpu.VMEM((CAP, HALF_D), jnp.uint32), pltpu.VMEM((CAP, D_MODEL), jnp.float32),
                        pltpu.SemaphoreType.DMA((2,))],
        compiler_params=_cparams(2),
        name="experts",
    )(idx[:, :, None, :], aff[:, :, None, :], hp, w1, w3, w2)


def _final_body(x_ref, delta_ref, mod_ref, g_ref, o_ref):
    x = x_ref[...] + _mod_row(mod_ref, MOD_GATE_F) * delta_ref[...]
    o_ref[...] = x * lax.rsqrt(jnp.mean(x * x, axis=-1, keepdims=True) + EPS) * g_ref[...]


def _final_norm(xc, delta, mods, g):
    B = xc.shape[0]
    lat_rows = pl.BlockSpec((None, ROW_TILE, D_MODEL), lambda b, t: (b, t + CTX_TILES, 0))
    return pl.pallas_call(
        _final_body,
        grid=(B, SEQ // ROW_TILE),
        in_specs=[lat_rows, lat_rows,
                  pl.BlockSpec((None, None, 6, D_MODEL), lambda b, t: (b, 1, 0, 0)),
                  pl.BlockSpec((1, D_MODEL), lambda b, t: (0, 0))],
        out_specs=pl.BlockSpec((None, ROW_TILE, D_MODEL), lambda b, t: (b, t, 0)),
        out_shape=jax.ShapeDtypeStruct((B, SEQ, D_MODEL), jnp.float32),
        compiler_params=_cparams(2),
        name="final_norm",
    )(xc, delta, mods, g)


def kernel(x, c, ctx, c_ctx, ada_w, ada_b, norm_mix_g, norm_ffn_g, w_in, hgrn_lb_logits,
           hgrn_onorm_g, mla_qnorm_g, mla_w_uq, mla_kvnorm_g, mla_w_ukv, na_rpb, w_out,
           router_w, exp_w1, exp_w3, exp_w2, final_norm_g):
    B = x.shape[0]
    lb_w = jax.nn.softmax(hgrn_lb_logits.astype(jnp.float32), axis=1)
    lb_all = jnp.cumsum(lb_w, axis=1) - lb_w[:, :1]
    s_lat = jax.nn.silu(c)
    s_ctx = jax.nn.silu(c_ctx)
    w_in_b = _widen_w_in(w_in)
    w_out_b = w_out.astype(jnp.bfloat16)
    wq, wqr, wk, wv = _mla_weights(mla_w_uq, mla_w_ukv)
    cos, sin = _rope_tables()
    w_router = jnp.pad(router_w, ((0, 0), (0, 0), (0, LANES - N_EXPERTS))).astype(jnp.bfloat16)
    w1_b, w3_b, w2_b = (w.astype(jnp.bfloat16) for w in (exp_w1, exp_w3, exp_w2))
    row = lambda g: g.reshape(1, -1).astype(jnp.float32)
    xc = jnp.concatenate([ctx, x], axis=1)
    delta, prev_mods = None, None
    for layer in range(DEPTH):
        mod_l = (s_lat @ ada_w[layer] + ada_b[layer]).reshape(B, 6, D_MODEL)
        mod_c = jnp.broadcast_to((s_ctx @ ada_w[layer] + ada_b[layer]).reshape(1, 6, D_MODEL), (B, 6, D_MODEL))
        mods = jnp.stack([mod_c, mod_l], axis=1)

        xc, p32, p16 = _in_proj(xc, delta, prev_mods, row(norm_mix_g[layer]), mods, w_in_b[layer])
        hg = _hgrn_mixer(p32, lb_all[:, layer], hgrn_onorm_g[layer])
        q, k, v = _mla_prep(p32, cos, sin, row(mla_qnorm_g[layer]), row(mla_kvnorm_g[layer]),
                            wq[layer], wqr[layer], wk[layer], wv[layer])
        ml = _mla_attention(q, k, v)
        na = _na_attention(p16, _na_bias_tables(na_rpb[layer]))
        xc, hp, aff = _out_proj(hg, ml, na, xc, mods, w_out_b[layer], row(norm_ffn_g[layer]), w_router[layer])
        delta = _experts(_route(aff), aff, hp, w1_b[layer], w3_b[layer], w2_b[layer])
        prev_mods = mods
    return _final_norm(xc, delta, prev_mods, row(final_norm_g))
```

```python
import functools
import math

import numpy as np
import jax
import jax.numpy as jnp
from jax import lax
from jax.experimental import pallas as pl
from jax.experimental.pallas import tpu as pltpu

D_MODEL = 1024
BATCH = 8
SEQ = 4096
DEPTH = 4

CTX_LEN = 256
GRID_W = 64
EPS = 1e-6
MIN_FORGET = 1e-20

D_MIX = D_MODEL
HG_HEADS = 4
HG_DIM = 64
HG_WIDTH = HG_HEADS * HG_DIM
HG_CHUNK = 64
MLA_HEADS = 6
MLA_NOPE = 64
MLA_ROPE = 32
MLA_V = 64
MLA_Q_RANK = 256
MLA_KV_RANK = 128
MLA_WIDTH = MLA_HEADS * MLA_V
NA_HEADS = 6
NA_DIM = 64
NA_WIDTH = NA_HEADS * NA_DIM
NA_WIN_R = 8
NA_WIN_C = 16
HG_IN = 5 * HG_WIDTH
MLA_IN = MLA_Q_RANK + MLA_KV_RANK + MLA_ROPE
NA_IN = 3 * NA_WIDTH
D_IN = HG_IN + MLA_IN + NA_IN
ROPE_BASE = 10000.0
N_EXPERTS = 16
EC_CAPACITY = 2
F_EXPERT = 1024

LANES = 128
SUBLANES = 8
VMEM_LIMIT_BYTES = 56 * 1024 * 1024

TOK = CTX_LEN + SEQ
ROW_TILE = 256
N_ROW_TILES = TOK // ROW_TILE
CTX_TILES = CTX_LEN // ROW_TILE

MLA_HEAD_PAD = LANES
P32_HG = 0
P32_CQ = HG_IN
P32_CKV = P32_CQ + MLA_Q_RANK
P32_KR = P32_CKV + MLA_KV_RANK
P32_KRP = P32_KR + LANES
P32_WIDTH = P32_KRP + LANES
P_WIDTH = P32_WIDTH + NA_IN


def _cparams(n_grid_dims):
    return pltpu.CompilerParams(dimension_semantics=("arbitrary",) * n_grid_dims,
                                vmem_limit_bytes=VMEM_LIMIT_BYTES)


def _dot_nt(a, b):
    return lax.dot_general(a, b, (((1,), (1,)), ((), ())), preferred_element_type=jnp.float32)


def _dot(a, b):
    return jnp.dot(a, b, preferred_element_type=jnp.float32)


def _kind_of_tile(t):
    return (t >= CTX_TILES).astype(jnp.int32)


MOD_SHIFT_A, MOD_SCALE_A, MOD_GATE_A, MOD_SHIFT_F, MOD_SCALE_F, MOD_GATE_F = range(6)


def _mod_row(mod_ref, i):
    return mod_ref[i:i + 1, :]


def _modulated_norm(x, g, shift, scale):
    y = x * lax.rsqrt(jnp.mean(x * x, axis=-1, keepdims=True) + EPS) * g
    return y * (1.0 + scale) + shift


def _in_proj_body(*refs, has_delta):
    if has_delta:
        x_ref, delta_ref, pmod_ref, g_ref, mod_ref, w_ref, xo_ref, p32_ref, p16_ref = refs
        x = x_ref[...] + _mod_row(pmod_ref, MOD_GATE_F) * delta_ref[...]
        xo_ref[...] = x
    else:
        x_ref, g_ref, mod_ref, w_ref, p32_ref, p16_ref = refs
        x = x_ref[...]
    h = _modulated_norm(x, g_ref[...], _mod_row(mod_ref, MOD_SHIFT_A), _mod_row(mod_ref, MOD_SCALE_A))
    p = _dot(h.astype(jnp.bfloat16), w_ref[...])
    p32_ref[...] = p[:, :P32_WIDTH]
    p16_ref[...] = p[:, P32_WIDTH:].astype(p16_ref.dtype)


def _row_spec(width):
    return pl.BlockSpec((None, ROW_TILE, width), lambda b, t: (b, t, 0))


def _mod_spec():
    return pl.BlockSpec((None, None, 6, D_MODEL), lambda b, t: (b, _kind_of_tile(t), 0, 0))


def _in_proj(xc, delta, prev_mods, g, mods, w):
    B = xc.shape[0]
    has_delta = delta is not None
    fixed = [pl.BlockSpec((1, D_MODEL), lambda b, t: (0, 0)), _mod_spec(),
             pl.BlockSpec((D_MODEL, P_WIDTH), lambda b, t: (0, 0))]
    outs = [_row_spec(P32_WIDTH), _row_spec(NA_IN)]
    shapes = [jax.ShapeDtypeStruct((B, TOK, P32_WIDTH), jnp.float32),
              jax.ShapeDtypeStruct((B, TOK, NA_IN), jnp.bfloat16)]
    if has_delta:
        in_specs = [_row_spec(D_MODEL), _row_spec(D_MODEL), _mod_spec()] + fixed
        args = (xc, delta, prev_mods, g, mods, w)
        outs = [_row_spec(D_MODEL)] + outs
        shapes = [jax.ShapeDtypeStruct((B, TOK, D_MODEL), jnp.float32)] + shapes
    else:
        in_specs = [_row_spec(D_MODEL)] + fixed
        args = (xc, g, mods, w)
    res = pl.pallas_call(
        functools.partial(_in_proj_body, has_delta=has_delta),
        grid=(B, N_ROW_TILES),
        in_specs=in_specs,
        out_specs=outs,
        out_shape=shapes,
        compiler_params=_cparams(2),
        name="in_proj",
    )(*args)
    return res if has_delta else [xc] + list(res)


def _rope_partner():
    half = MLA_ROPE // 2
    quarter = half // 2
    partner, sign = [], []
    for r in range(MLA_ROPE):
        base, j = (r // half) * half, r % half
        partner.append(base + (j + quarter if j < quarter else j - quarter))
        sign.append(-1.0 if j < quarter else 1.0)
    return np.array(partner), np.array(sign, np.float32)


def _widen_w_in(w_in):
    L = w_in.shape[0]
    a0, a1 = HG_IN, HG_IN + MLA_IN
    partner, sign = _rope_partner()
    kr = w_in[..., a0 + MLA_Q_RANK + MLA_KV_RANK:a1]
    zl = jnp.zeros((L, D_MODEL, MLA_NOPE), w_in.dtype)
    zr = jnp.zeros((L, D_MODEL, MLA_HEAD_PAD - MLA_NOPE - MLA_ROPE), w_in.dtype)
    na = w_in[..., a1:]
    return jnp.concatenate([
        w_in[..., :a0 + MLA_Q_RANK + MLA_KV_RANK],
        zl, kr, zr,
        zl, kr[..., partner] * sign, zr,
        na[..., :NA_WIDTH] * NA_DIM ** -0.5, na[..., NA_WIDTH:]], axis=-1).astype(jnp.bfloat16)


GRID_H = SEQ // GRID_W
NA_QROWS = 2
NA_QB = NA_QROWS * GRID_W
NA_KROWS = NA_QROWS + NA_WIN_R - 1
NA_KB = NA_KROWS * GRID_W
NA_NBLK = GRID_H // NA_QROWS
NA_MASKED = -1e30


def _na_window_start(m):
    return min(max(NA_QROWS * m - NA_WIN_R // 2, 0), GRID_H - NA_KROWS)


NA_NDROW = 2 * NA_WIN_R - 1
NA_NDCOL = 2 * NA_WIN_C - 1


@functools.lru_cache(maxsize=None)
def _na_patterns():
    pats, pat_of_block = [], []
    for m in range(NA_NBLK):
        pat = np.full((NA_QROWS, NA_KROWS), NA_NDROW, np.int64)
        for qr in range(NA_QROWS):
            r = NA_QROWS * m + qr
            rs = min(max(r - NA_WIN_R // 2, 0), GRID_H - NA_WIN_R)
            for j in range(NA_KROWS):
                kr = _na_window_start(m) + j
                if rs <= kr < rs + NA_WIN_R:
                    pat[qr, j] = kr - r + NA_WIN_R - 1
        for p, other in enumerate(pats):
            if (other == pat).all():
                pat_of_block.append(p)
                break
        else:
            pat_of_block.append(len(pats))
            pats.append(pat)
    return np.stack(pats), tuple(pat_of_block)


def _na_bias_tables(rpb):
    pats, _ = _na_patterns()
    c = np.arange(GRID_W)[:, None]
    kc = np.arange(GRID_W)[None, :]
    cs = np.clip(c - NA_WIN_C // 2, 0, GRID_W - NA_WIN_C)
    col_valid = (kc >= cs) & (kc < cs + NA_WIN_C)
    onehot = (np.arange(NA_NDCOL)[:, None, None] == (kc - c + NA_WIN_C - 1)[None]).astype(np.float32)
    tiles = jnp.einsum('hdj,jck->hdck', rpb.astype(jnp.float32), onehot, precision=lax.Precision.HIGHEST)
    tiles = jnp.where(col_valid, tiles, NA_MASKED)
    tiles = jnp.concatenate([tiles, jnp.full((NA_HEADS, 1, GRID_W, GRID_W), NA_MASKED, jnp.float32)], axis=1)
    rows = [jnp.concatenate([tiles[:, int(d)] for d in pat_row], axis=-1) for pat in pats for pat_row in pat]
    return jnp.stack(rows, axis=1).reshape(NA_HEADS, len(pats), NA_QB, NA_KB)


def _na_body(q_ref, k_ref, v_ref, bias_ref, o_ref):
    _, pat_of_block = _na_patterns()
    lane = lax.broadcasted_iota(jnp.int32, (1, LANES), 1)
    head_masks = [(lane >= NA_DIM * hh) & (lane < NA_DIM * (hh + 1)) for hh in range(2)]
    kc = k_ref[0:CTX_LEN, :]
    vc = v_ref[0:CTX_LEN, :]
    vc_h = [jnp.where(hm, vc, jnp.zeros_like(vc)) for hm in head_masks]

    def block(m, ws, pat):
        qrows = pl.ds(pl.multiple_of(CTX_LEN + m * NA_QB, NA_QB), NA_QB)
        q = q_ref[qrows, :]
        krows = pl.ds(pl.multiple_of(CTX_LEN + ws * GRID_W, GRID_W), NA_KB)
        kwin = k_ref[krows, :]
        vwin = v_ref[krows, :]
        out = None
        for hh in range(2):
            qh = jnp.where(head_masks[hh], q, jnp.zeros_like(q))
            s_w = _dot_nt(qh, kwin) + bias_ref[hh, pat]
            s_c = _dot_nt(qh, kc)
            mx = jnp.maximum(jnp.max(s_w, axis=-1, keepdims=True), jnp.max(s_c, axis=-1, keepdims=True))
            p_w = jnp.exp(s_w - mx)
            p_c = jnp.exp(s_c - mx)
            den = jnp.sum(p_w, axis=-1, keepdims=True) + jnp.sum(p_c, axis=-1, keepdims=True)
            vh = jnp.where(head_masks[hh], vwin, jnp.zeros_like(vwin))
            o = _dot(p_w.astype(jnp.bfloat16), vh) + _dot(p_c.astype(jnp.bfloat16), vc_h[hh])
            o = o / den
            out = o if out is None else out + o
        o_ref[qrows, :] = out.astype(o_ref.dtype)

    interior = [m for m in range(NA_NBLK) if pat_of_block.count(pat_of_block[m]) > 1]
    lo, hi = interior[0], interior[-1] + 1
    assert interior == list(range(lo, hi)) and len({pat_of_block[m] for m in interior}) == 1
    for m in list(range(lo)) + list(range(hi, NA_NBLK)):
        block(m, _na_window_start(m), pat_of_block[m])

    def loop_body(m, carry):
        block(m, NA_QROWS * m - NA_WIN_R // 2, pat_of_block[lo])
        return carry

    lax.fori_loop(lo, hi, loop_body, 0, unroll=2)

    qc = q_ref[0:CTX_LEN, :]
    out = None
    for hh in range(2):
        qh = jnp.where(head_masks[hh], qc, jnp.zeros_like(qc))
        s = _dot_nt(qh, kc)
        p = jnp.exp(s - jnp.max(s, axis=-1, keepdims=True))
        o = _dot(p.astype(jnp.bfloat16), vc_h[hh]) / jnp.sum(p, axis=-1, keepdims=True)
        out = o if out is None else out + o
    o_ref[0:CTX_LEN, :] = out.astype(o_ref.dtype)


def _na_attention(p16, bias):
    B = p16.shape[0]
    n_pat = bias.shape[1]
    n_pairs = NA_WIDTH // LANES

    def col(group):
        return pl.BlockSpec((None, TOK, LANES), lambda b, hp, group=group: (b, 0, group * n_pairs + hp))

    return pl.pallas_call(
        _na_body,
        grid=(B, n_pairs),
        in_specs=[col(0), col(1), col(2),
                  pl.BlockSpec((2, n_pat, NA_QB, NA_KB), lambda b, hp: (hp, 0, 0, 0))],
        out_specs=col(0),
        out_shape=jax.ShapeDtypeStruct((B, TOK, NA_WIDTH), jnp.bfloat16),
        compiler_params=_cparams(2),
        name="na_attention",
    )(p16, p16, p16, bias)


HG_NCHUNK = TOK // HG_CHUNK
HG_CTX_CHUNKS = CTX_LEN // HG_CHUNK


def _split3_bf16(x):
    hi = x.astype(jnp.bfloat16)
    r1 = x - hi.astype(jnp.float32)
    mid = r1.astype(jnp.bfloat16)
    lo = (r1 - mid.astype(jnp.float32)).astype(jnp.bfloat16)
    return hi, mid, lo


HG_CHUNK_SCRATCH = 5


def _hgrn_body(q_ref, i_ref, zf_ref, zb_ref, g_ref, lb_ref, gn_ref, o_ref, of_s, ob_s, *chunk_scratch):
    scratch = (chunk_scratch[:HG_CHUNK_SCRATCH], chunk_scratch[HG_CHUNK_SCRATCH:])
    C = HG_CHUNK
    r_i = lax.broadcasted_iota(jnp.int32, (C, C), 0)
    c_i = lax.broadcasted_iota(jnp.int32, (C, C), 1)
    tri = {True: (c_i <= r_i).astype(jnp.bfloat16), False: (c_i >= r_i).astype(jnp.bfloat16)}
    r2 = lax.broadcasted_iota(jnp.int32, (LANES, LANES), 0) // HG_DIM
    c2 = lax.broadcasted_iota(jnp.int32, (LANES, LANES), 1) // HG_DIM
    same_head = r2 == c2
    half_ones = same_head.astype(jnp.bfloat16)
    row_id = lax.broadcasted_iota(jnp.int32, (C, LANES), 0)
    lane_lo = lax.broadcasted_iota(jnp.int32, (1, LANES), 1) < HG_DIM

    def chunk(c_idx, s_t, fwd, bufs):
        b_s, k_s, q_s, v_s, oi_s = bufs
        rows = pl.ds(pl.multiple_of(c_idx * C, C), C)
        lb = lb_ref[0:1, :] if fwd else lb_ref[1:2, :]
        z = (zf_ref if fwd else zb_ref)[rows, :]
        f = jnp.maximum(lb + (1.0 - lb) * jax.nn.sigmoid(z), MIN_FORGET)
        lf = jnp.log(f)
        hi, mid, lo = _split3_bf16(lf)
        b = _dot(tri[fwd], hi) + _dot(tri[fwd], mid) + _dot(tri[fwd], lo)
        q = q_ref[rows, :] * (HG_DIM ** -0.5)
        v = i_ref[rows, :]
        b_s[...] = b
        k_s[...] = 1.0 - f
        q_s[...] = q
        v_s[...] = v
        b_last = b[C - 1:C, :] if fwd else b[0:1, :]
        k = 1.0 - f
        G = SUBLANES
        n_groups = C // G
        ref = (lambda g: G * g + G - 1) if fwd else (lambda g: G * g)
        later = (lambda g: slice(G * (g + 1), C)) if fwd else (lambda g: slice(0, G * g))

        tiles = []
        for i in range(C):
            grp = slice(i // G * G, i // G * G + G)
            e = jnp.exp(jnp.minimum(b_s[i:i + 1, :] - b_s[grp, :], 0.0))
            visible = (row_id[0:G, :] <= i % G) if fwd else (row_id[0:G, :] >= i % G)
            tiles.append(jnp.where(visible, e * k_s[grp, :] * q_s[i:i + 1, :], 0.0))
        a_same = _dot(jnp.concatenate(tiles, axis=0).astype(jnp.bfloat16), half_ones)
        for i in range(C):
            grp = slice(i // G * G, i // G * G + G)
            oi_s[i:i + 1, :] = jnp.sum(a_same[G * i:G * i + G, :] * v_s[grp, :], axis=0, keepdims=True)

        b_ref_rows = jnp.concatenate([jnp.broadcast_to(b[ref(g):ref(g) + 1, :], (G, LANES))
                                      for g in range(n_groups)], axis=0)
        kp_t = (k * jnp.exp(b_ref_rows - b)).T
        rhs = jnp.where(same_head, jnp.concatenate([kp_t, kp_t], axis=1), 0.0)
        key_groups = [g for g in range(n_groups) if later(g).start < later(g).stop]
        lhs = jnp.concatenate([q[later(g), :] * jnp.exp(b[later(g), :] - b[ref(g):ref(g) + 1, :])
                               for g in key_groups], axis=0)
        res = _dot(lhs.astype(jnp.bfloat16), rhs.astype(jnp.bfloat16))
        lane_group = (lax.broadcasted_iota(jnp.int32, (G, LANES), 1) % HG_DIM) // G
        offs = np.cumsum([0] + [later(g).stop - later(g).start for g in key_groups])
        a_rows = []
        for blk in range(n_groups):
            acc = jnp.zeros((G, LANES), jnp.float32)
            for n, g in enumerate(key_groups):
                if later(g).start <= G * blk < later(g).stop:
                    r0 = int(offs[n]) + G * blk - later(g).start
                    acc = jnp.where(lane_group == g, res[r0:r0 + G, :], acc)
            a_rows.append(acc)
        a_cross = jnp.concatenate(a_rows, axis=0).astype(jnp.bfloat16)
        v_bd = jnp.where(same_head, jnp.concatenate([v, v], axis=0), 0.0).astype(jnp.bfloat16)

        qe = (q * jnp.exp(b)).astype(jnp.bfloat16)
        o = oi_s[...] + _dot(a_cross, v_bd) + _dot_nt(qe, s_t.astype(jnp.bfloat16))
        ke = (k_s[...] * jnp.exp(b_last - b)).astype(jnp.bfloat16)
        upd = _dot(v.T.astype(jnp.bfloat16), ke)
        s_t = s_t * jnp.exp(b_last) + jnp.where(same_head, upd, 0.0)
        return rows, o, s_t

    def scan_body(n, states):
        rows, o, s_fwd = chunk(n, states[0], True, scratch[0])
        of_s[rows, :] = o
        c_idx = jnp.where(n < HG_CTX_CHUNKS, HG_CTX_CHUNKS - 1 - n, HG_NCHUNK + HG_CTX_CHUNKS - 1 - n)
        rows, o, s_bwd = chunk(c_idx, states[1], False, scratch[1])
        ob_s[rows, :] = o
        return s_fwd, s_bwd

    def readout_body(n, carry):
        rows = pl.ds(pl.multiple_of(n * C, C), C)
        o = of_s[rows, :] + ob_s[rows, :]
        sq = o * o
        ms = jnp.where(lane_lo,
                       jnp.sum(jnp.where(lane_lo, sq, 0.0), axis=-1, keepdims=True),
                       jnp.sum(jnp.where(lane_lo, 0.0, sq), axis=-1, keepdims=True)) * (1.0 / HG_DIM)
        y = o * lax.rsqrt(ms + EPS) * gn_ref[...]
        g = g_ref[rows, :]
        o_ref[rows, :] = (y * (g * jax.nn.sigmoid(g))).astype(o_ref.dtype)
        return carry

    zero = jnp.zeros((LANES, LANES), jnp.float32)
    lax.fori_loop(0, HG_NCHUNK, scan_body, (zero, zero), unroll=2)
    lax.fori_loop(0, HG_NCHUNK, readout_body, 0)


def _hgrn_mixer(p32, lb, onorm_g):
    B = p32.shape[0]
    n_pairs = HG_WIDTH // LANES

    def col(group):
        return pl.BlockSpec((None, TOK, LANES), lambda b, hp, group=group: (b, 0, group * n_pairs + hp))

    gn = jnp.tile(onorm_g.astype(jnp.float32), LANES // HG_DIM).reshape(1, LANES)
    chunk_scratch = pltpu.VMEM((HG_CHUNK, LANES), jnp.float32)
    return pl.pallas_call(
        _hgrn_body,
        grid=(B, n_pairs),
        in_specs=[col(0), col(1), col(2), col(3), col(4),
                  pl.BlockSpec((2, LANES), lambda b, hp: (0, hp)),
                  pl.BlockSpec((1, LANES), lambda b, hp: (0, 0))],
        out_specs=pl.BlockSpec((None, TOK, LANES), lambda b, hp: (b, 0, hp)),
        out_shape=jax.ShapeDtypeStruct((B, TOK, HG_WIDTH), jnp.bfloat16),
        scratch_shapes=[pltpu.VMEM((TOK, LANES), jnp.float32)] * 2 + [chunk_scratch] * (2 * HG_CHUNK_SCRATCH),
        compiler_params=_cparams(2),
        name="hgrn2",
    )(p32, p32, p32, p32, p32, lb.astype(jnp.float32), gn)


MLA_QK_WIDTH = MLA_HEADS * MLA_HEAD_PAD
MLA_QSCALE = (MLA_NOPE + MLA_ROPE) ** -0.5 * math.log2(math.e)
MLA_TQ = 256


def _mla_prep_body(cq_ref, ckv_ref, kr_ref, krp_ref, cos_ref, sin_ref, gq_ref, gkv_ref,
                   wq_ref, wqr_ref, wk_ref, wv_ref, q_out, k_out, v_out):
    def rms(x, g):
        return (x * lax.rsqrt(jnp.mean(x * x, axis=-1, keepdims=True) + EPS) * g).astype(jnp.bfloat16)

    cos = cos_ref[...]
    sin = sin_ref[...]
    cqn = rms(cq_ref[...], gq_ref[...])
    q = _dot(cqn, wq_ref[...])
    qr = _dot(cqn, wqr_ref[...])
    ckvn = rms(ckv_ref[...], gkv_ref[...])
    kn = _dot(ckvn, wk_ref[...])
    v_out[...] = _dot(ckvn, wv_ref[...]).astype(v_out.dtype)
    k_rope = kr_ref[...] * cos + krp_ref[...] * sin
    for h in range(MLA_HEADS):
        sl = slice(h * MLA_HEAD_PAD, (h + 1) * MLA_HEAD_PAD)
        q_out[:, sl] = ((q[:, sl] * cos + qr[:, sl] * sin) * MLA_QSCALE).astype(q_out.dtype)
        k_out[:, sl] = (kn[:, sl] + k_rope).astype(k_out.dtype)


def _mla_weights(w_uq, w_ukv):
    partner, sign = _rope_partner()
    L = w_uq.shape[0]
    wq = w_uq.reshape(L, MLA_Q_RANK, MLA_HEADS, MLA_NOPE + MLA_ROPE)
    q_nope, q_rope = wq[..., :MLA_NOPE], wq[..., MLA_NOPE:]
    pad = jnp.zeros(wq.shape[:3] + (MLA_HEAD_PAD - MLA_NOPE - MLA_ROPE,), w_uq.dtype)
    wq_pad = jnp.concatenate([q_nope, q_rope, pad], axis=-1)
    wq_rot = jnp.concatenate([jnp.zeros_like(q_nope), q_rope[..., partner] * sign, pad], axis=-1)
    wkv = w_ukv.reshape(L, MLA_KV_RANK, MLA_HEADS, MLA_NOPE + MLA_V)
    k_nope, v = wkv[..., :MLA_NOPE], wkv[..., MLA_NOPE:]
    wk_pad = jnp.concatenate([k_nope, jnp.zeros(k_nope.shape[:3] + (MLA_HEAD_PAD - MLA_NOPE,), w_ukv.dtype)], axis=-1)
    bf = lambda a, n: a.reshape(L, a.shape[1], n).astype(jnp.bfloat16)
    return bf(wq_pad, MLA_QK_WIDTH), bf(wq_rot, MLA_QK_WIDTH), bf(wk_pad, MLA_QK_WIDTH), bf(v, MLA_WIDTH)


def _rope_tables():
    t = jnp.arange(SEQ)
    half = MLA_ROPE // 2
    inv = ROPE_BASE ** (-jnp.arange(0, half, 2, dtype=jnp.float32) / half)
    ang_r = (t // GRID_W).astype(jnp.float32)[:, None] * inv
    ang_c = (t % GRID_W).astype(jnp.float32)[:, None] * inv
    ang = jnp.concatenate([ang_r, ang_r, ang_c, ang_c], axis=-1)
    ones = jnp.ones((SEQ, MLA_NOPE), jnp.float32)
    zeros_n = jnp.zeros((SEQ, MLA_NOPE), jnp.float32)
    zeros_p = jnp.zeros((SEQ, MLA_HEAD_PAD - MLA_NOPE - MLA_ROPE), jnp.float32)
    cos_l = jnp.concatenate([ones, jnp.cos(ang), zeros_p], axis=-1)
    sin_l = jnp.concatenate([zeros_n, jnp.sin(ang), zeros_p], axis=-1)
    lane = np.arange(MLA_HEAD_PAD)
    cos_c = jnp.broadcast_to(jnp.asarray((lane < MLA_NOPE + MLA_ROPE).astype(np.float32)), (CTX_LEN, MLA_HEAD_PAD))
    sin_c = jnp.zeros((CTX_LEN, MLA_HEAD_PAD), jnp.float32)
    return jnp.concatenate([cos_c, cos_l], axis=0), jnp.concatenate([sin_c, sin_l], axis=0)


def _mla_prep(p32, cos, sin, gq, gkv, wq, wqr, wk, wv):
    B = p32.shape[0]
    full = lambda a: pl.BlockSpec(a.shape, lambda b, t: (0,) * a.ndim)
    rows = lambda width, blk: pl.BlockSpec((None, ROW_TILE, width), lambda b, t, blk=blk: (b, t, blk))
    tab = pl.BlockSpec((ROW_TILE, MLA_HEAD_PAD), lambda b, t: (t, 0))
    return pl.pallas_call(
        _mla_prep_body,
        grid=(B, N_ROW_TILES),
        in_specs=[rows(MLA_Q_RANK, P32_CQ // MLA_Q_RANK), rows(MLA_KV_RANK, P32_CKV // MLA_KV_RANK),
                  rows(LANES, P32_KR // LANES), rows(LANES, P32_KRP // LANES), tab, tab,
                  full(gq), full(gkv), full(wq), full(wqr), full(wk), full(wv)],
        out_specs=[rows(MLA_QK_WIDTH, 0), rows(MLA_QK_WIDTH, 0), rows(MLA_WIDTH, 0)],
        out_shape=[jax.ShapeDtypeStruct((B, TOK, MLA_QK_WIDTH), jnp.bfloat16),
                   jax.ShapeDtypeStruct((B, TOK, MLA_QK_WIDTH), jnp.bfloat16),
                   jax.ShapeDtypeStruct((B, TOK, MLA_WIDTH), jnp.bfloat16)],
        compiler_params=_cparams(2),
        name="mla_prep",
    )(p32, p32, p32, p32, cos, sin, gq, gkv, wq, wqr, wk, wv)


def _mla_attn_body(q_ref, k_ref, v_ref, o_ref, vm_s):
    lane = lax.broadcasted_iota(jnp.int32, (1, LANES), 1)
    v = v_ref[...]
    for hh in range(2):
        vm_s[hh] = jnp.where((lane >= MLA_V * hh) & (lane < MLA_V * (hh + 1)), v, jnp.zeros_like(v))

    def attend(qrows, n_keys):
        out = None
        for hh in range(2):
            sl = slice(hh * MLA_HEAD_PAD, (hh + 1) * MLA_HEAD_PAD)
            s = _dot_nt(q_ref[qrows, sl], k_ref[0:n_keys, sl])
            p = jnp.exp2(s - jnp.max(s, axis=-1, keepdims=True))
            den = jnp.sum(p, axis=-1, keepdims=True)
            o = _dot(p.astype(jnp.bfloat16), vm_s[hh, 0:n_keys, :]) / den
            out = o if out is None else out + o
        o_ref[qrows, :] = out.astype(o_ref.dtype)

    def loop_body(j, carry):
        attend(pl.ds(pl.multiple_of(CTX_LEN + j * MLA_TQ, MLA_TQ), MLA_TQ), TOK)
        return carry

    lax.fori_loop(0, SEQ // MLA_TQ, loop_body, 0)
    attend(slice(0, CTX_LEN), CTX_LEN)


def _mla_attention(q, k, v):
    B = q.shape[0]
    qk = pl.BlockSpec((None, TOK, 2 * MLA_HEAD_PAD), lambda b, hp: (b, 0, hp))
    vo = pl.BlockSpec((None, TOK, LANES), lambda b, hp: (b, 0, hp))
    return pl.pallas_call(
        _mla_attn_body,
        grid=(B, MLA_WIDTH // LANES),
        in_specs=[qk, qk, vo],
        out_specs=vo,
        out_shape=jax.ShapeDtypeStruct((B, TOK, MLA_WIDTH), jnp.bfloat16),
        scratch_shapes=[pltpu.VMEM((2, TOK, LANES), jnp.bfloat16)],
        compiler_params=_cparams(2),
        name="mla_attention",
    )(q, k, v)


HALF_D = D_MODEL // 2
HI16 = 0xFFFF0000


def _out_proj_body(hg_ref, ml_ref, na_ref, x_ref, mod_ref, w_ref, gf_ref, wr_ref, o_ref, hp_ref, aff_ref):
    o = (_dot(hg_ref[...], w_ref[0:HG_WIDTH, :])
         + _dot(ml_ref[...], w_ref[HG_WIDTH:HG_WIDTH + MLA_WIDTH, :])
         + _dot(na_ref[...], w_ref[HG_WIDTH + MLA_WIDTH:, :]))
    x = x_ref[...] + _mod_row(mod_ref, MOD_GATE_A) * o
    o_ref[...] = x
    h = _modulated_norm(x, gf_ref[...], _mod_row(mod_ref, MOD_SHIFT_F), _mod_row(mod_ref, MOD_SCALE_F))
    hb = h.astype(jnp.bfloat16)
    bits = pltpu.bitcast(hb.astype(jnp.float32), jnp.uint32)
    hp_ref[...] = (bits[:, HALF_D:] & jnp.uint32(HI16)) | (bits[:, :HALF_D] >> 16)
    logits = _dot(hb, wr_ref[...])
    lane = lax.broadcasted_iota(jnp.int32, logits.shape, 1)
    logits = jnp.where(lane < N_EXPERTS, logits, NA_MASKED)
    e = jnp.exp(logits - jnp.max(logits, axis=-1, keepdims=True))
    aff = e / jnp.sum(e, axis=-1, keepdims=True)
    aff_ref[...] = aff.T[0:N_EXPERTS, :]


def _out_proj(hg, ml, na, xc, mods, w, g_ffn, w_router):
    B = xc.shape[0]
    return pl.pallas_call(
        _out_proj_body,
        grid=(B, N_ROW_TILES),
        in_specs=[_row_spec(HG_WIDTH), _row_spec(MLA_WIDTH), _row_spec(NA_WIDTH), _row_spec(D_MODEL),
                  _mod_spec(),
                  pl.BlockSpec((D_MIX, D_MODEL), lambda b, t: (0, 0)),
                  pl.BlockSpec((1, D_MODEL), lambda b, t: (0, 0)),
                  pl.BlockSpec((D_MODEL, LANES), lambda b, t: (0, 0))],
        out_specs=[_row_spec(D_MODEL), _row_spec(HALF_D),
                   pl.BlockSpec((None, N_EXPERTS, ROW_TILE), lambda b, t: (b, 0, t))],
        out_shape=[jax.ShapeDtypeStruct((B, TOK, D_MODEL), jnp.float32),
                   jax.ShapeDtypeStruct((B, TOK, HALF_D), jnp.uint32),
                   jax.ShapeDtypeStruct((B, N_EXPERTS, TOK), jnp.float32)],
        compiler_params=_cparams(2),
        name="out_proj",
    )(hg, ml, na, xc, mods, w, g_ffn, w_router)


CAP_CTX = EC_CAPACITY * CTX_LEN // N_EXPERTS
CAP_LAT = EC_CAPACITY * SEQ // N_EXPERTS
CAP = CAP_CTX + CAP_LAT
IDX_PAD = -(-CAP // LANES) * LANES
ONE_BITS = 0x3F800000


def _prefix_count(m, upper):
    outs, off = [], jnp.zeros((m.shape[0], 1), jnp.float32)
    for j in range(m.shape[1] // LANES):
        loc = _dot(m[:, j * LANES:(j + 1) * LANES].astype(jnp.bfloat16), upper)
        outs.append(loc + off)
        off = off + loc[:, LANES - 1:LANES]
    return jnp.concatenate(outs, axis=1)


def _select_top(bits, k, upper):
    rows = bits.shape[0]

    def step(_, lohi):
        lo, hi = lohi
        mid = lo + ((hi - lo + 1) >> 1)
        cnt = jnp.sum((bits >= mid).astype(jnp.int32), axis=-1, keepdims=True)
        ok = cnt >= k
        return jnp.where(ok, mid, lo), jnp.where(ok, hi, mid - 1)

    thr, _ = lax.fori_loop(0, 31, step, (jnp.zeros((rows, 1), jnp.int32), jnp.full((rows, 1), ONE_BITS, jnp.int32)))
    gt = (bits > thr).astype(jnp.float32)
    eq = (bits == thr).astype(jnp.float32)
    need = k - jnp.sum(gt, axis=-1, keepdims=True)
    return gt + eq * (_prefix_count(eq, upper) <= need).astype(jnp.float32)


def _route_body(aff_ref, idx_ref):
    bits = pltpu.bitcast(aff_ref[...], jnp.int32)
    r = lax.broadcasted_iota(jnp.int32, (LANES, LANES), 0)
    c = lax.broadcasted_iota(jnp.int32, (LANES, LANES), 1)
    upper = (r <= c).astype(jnp.bfloat16)
    sel = jnp.concatenate([_select_top(bits[:, :CTX_LEN], CAP_CTX, upper),
                           _select_top(bits[:, CTX_LEN:], CAP_LAT, upper)], axis=1)
    pos = _prefix_count(sel, upper)
    lane = lax.broadcasted_iota(jnp.int32, (LANES, LANES), 1)
    for sb in range(IDX_PAD // LANES):
        slot = (lax.broadcasted_iota(jnp.int32, (LANES, 1), 0) + sb * LANES).astype(jnp.float32)
        cols = jnp.zeros((LANES, LANES), jnp.float32)
        for e in range(N_EXPERTS):
            cnt = jnp.sum((pos[e:e + 1, :] <= slot).astype(jnp.float32), axis=-1, keepdims=True)
            cols = jnp.where(lane == e, cnt, cols)
        idx = jnp.minimum(cols.T[0:N_EXPERTS, :], TOK - 1)
        idx_ref[:, sb * LANES:(sb + 1) * LANES] = idx.astype(jnp.int32)


def _route(aff):
    B = aff.shape[0]
    return pl.pallas_call(
        _route_body,
        grid=(B,),
        in_specs=[pl.BlockSpec((None, N_EXPERTS, TOK), lambda b: (b, 0, 0))],
        out_specs=pl.BlockSpec((None, N_EXPERTS, IDX_PAD), lambda b: (b, 0, 0)),
        out_shape=jax.ShapeDtypeStruct((B, N_EXPERTS, IDX_PAD), jnp.int32),
        compiler_params=_cparams(1),
        name="route",
    )(aff)


ROW_UNROLL = 8
SCATTER_BATCH = 4


def _experts_body(idx_ref, aff_ref, hp_hbm, w1_ref, w3_ref, w2_ref, delta_hbm, hp_s, acc_s, xs_s, y_s, sem):
    b = pl.program_id(0)
    e = pl.program_id(1)

    @pl.when(e == 0)
    def _():
        load = pltpu.make_async_copy(hp_hbm.at[b], hp_s, sem.at[0])
        load.start()
        acc_s[...] = jnp.zeros_like(acc_s)
        load.wait()

    def gather(s, carry):
        xs_s[pl.ds(s, 1), :] = hp_s[pl.ds(idx_ref[0, s], 1), :]
        return carry

    lax.fori_loop(0, CAP, gather, 0, unroll=ROW_UNROLL)

    xs = xs_s[...]
    lo = pltpu.bitcast(xs << 16, jnp.float32).astype(jnp.bfloat16)
    hi = pltpu.bitcast(xs & jnp.uint32(HI16), jnp.float32).astype(jnp.bfloat16)
    a = _dot(lo, w1_ref[0:HALF_D, :]) + _dot(hi, w1_ref[HALF_D:, :])
    u = _dot(lo, w3_ref[0:HALF_D, :]) + _dot(hi, w3_ref[HALF_D:, :])
    mid = (a * jax.nn.sigmoid(a) * u).astype(jnp.bfloat16)
    y_s[...] = _dot(mid, w2_ref[...])

    def scatter(n, carry):
        rows = [idx_ref[0, n * SCATTER_BATCH + j] for j in range(SCATTER_BATCH)]
        new = [acc_s[pl.ds(rows[j], 1), :] + aff_ref[0, rows[j]] * y_s[pl.ds(n * SCATTER_BATCH + j, 1), :]
               for j in range(SCATTER_BATCH)]
        for j in range(SCATTER_BATCH):
            acc_s[pl.ds(rows[j], 1), :] = new[j]
        return carry

    lax.fori_loop(0, CAP // SCATTER_BATCH, scatter, 0, unroll=2)

    @pl.when(e == N_EXPERTS - 1)
    def _():
        store = pltpu.make_async_copy(acc_s, delta_hbm.at[b], sem.at[1])
        store.start()
        store.wait()


def _experts(idx, aff, hp, w1, w3, w2):
    B = hp.shape[0]
    smem = lambda n: pl.BlockSpec((None, None, 1, n), lambda b, e: (b, e, 0, 0), memory_space=pltpu.SMEM)
    wspec = lambda k, n: pl.BlockSpec((None, k, n), lambda b, e: (e, 0, 0))
    return pl.pallas_call(
        _experts_body,
        grid=(B, N_EXPERTS),
        in_specs=[smem(IDX_PAD), smem(TOK), pl.BlockSpec(memory_space=pl.ANY),
                  wspec(D_MODEL, F_EXPERT), wspec(D_MODEL, F_EXPERT), wspec(F_EXPERT, D_MODEL)],
        out_specs=pl.BlockSpec(memory_space=pl.ANY),
        out_shape=jax.ShapeDtypeStruct((B, TOK, D_MODEL), jnp.float32),
        scratch_shapes=[pltpu.VMEM((TOK, HALF_D), jnp.uint32), pltpu.VMEM((TOK, D_MODEL), jnp.float32),
                        pltpu.VMEM((CAP, HALF_D), jnp.uint32), pltpu.VMEM((CAP, D_MODEL), jnp.float32),
                        pltpu.SemaphoreType.DMA((2,))],
        compiler_params=_cparams(2),
        name="experts",
    )(idx[:, :, None, :], aff[:, :, None, :], hp, w1, w3, w2)


def _final_body(x_ref, delta_ref, mod_ref, g_ref, o_ref):
    x = x_ref[...] + _mod_row(mod_ref, MOD_GATE_F) * delta_ref[...]
    o_ref[...] = x * lax.rsqrt(jnp.mean(x * x, axis=-1, keepdims=True) + EPS) * g_ref[...]


def _final_norm(xc, delta, mods, g):
    B = xc.shape[0]
    lat_rows = pl.BlockSpec((None, ROW_TILE, D_MODEL), lambda b, t: (b, t + CTX_TILES, 0))
    return pl.pallas_call(
        _final_body,
        grid=(B, SEQ // ROW_TILE),
        in_specs=[lat_rows, lat_rows,
                  pl.BlockSpec((None, None, 6, D_MODEL), lambda b, t: (b, 1, 0, 0)),
                  pl.BlockSpec((1, D_MODEL), lambda b, t: (0, 0))],
        out_specs=pl.BlockSpec((None, ROW_TILE, D_MODEL), lambda b, t: (b, t, 0)),
        out_shape=jax.ShapeDtypeStruct((B, SEQ, D_MODEL), jnp.float32),
        compiler_params=_cparams(2),
        name="final_norm",
    )(xc, delta, mods, g)


def kernel(x, c, ctx, c_ctx, ada_w, ada_b, norm_mix_g, norm_ffn_g, w_in, hgrn_lb_logits,
           hgrn_onorm_g, mla_qnorm_g, mla_w_uq, mla_kvnorm_g, mla_w_ukv, na_rpb, w_out,
           router_w, exp_w1, exp_w3, exp_w2, final_norm_g):
    B = x.shape[0]
    lb_w = jax.nn.softmax(hgrn_lb_logits.astype(jnp.float32), axis=1)
    lb_all = jnp.cumsum(lb_w, axis=1) - lb_w[:, :1]
    s_lat = jax.nn.silu(c)
    s_ctx = jax.nn.silu(c_ctx)
    w_in_b = _widen_w_in(w_in)
    w_out_b = w_out.astype(jnp.bfloat16)
    wq, wqr, wk, wv = _mla_weights(mla_w_uq, mla_w_ukv)
    cos, sin = _rope_tables()
    w_router = jnp.pad(router_w, ((0, 0), (0, 0), (0, LANES - N_EXPERTS))).astype(jnp.bfloat16)
    w1_b, w3_b, w2_b = (w.astype(jnp.bfloat16) for w in (exp_w1, exp_w3, exp_w2))
    row = lambda g: g.reshape(1, -1).astype(jnp.float32)
    xc = jnp.concatenate([ctx, x], axis=1)
    delta, prev_mods = None, None
    for layer in range(DEPTH):
        mod_l = (s_lat @ ada_w[layer] + ada_b[layer]).reshape(B, 6, D_MODEL)
        mod_c = jnp.broadcast_to((s_ctx @ ada_w[layer] + ada_b[layer]).reshape(1, 6, D_MODEL), (B, 6, D_MODEL))
        mods = jnp.stack([mod_c, mod_l], axis=1)

        xc, p32, p16 = _in_proj(xc, delta, prev_mods, row(norm_mix_g[layer]), mods, w_in_b[layer])
        hg = _hgrn_mixer(p32, lb_all[:, layer], hgrn_onorm_g[layer])
        q, k, v = _mla_prep(p32, cos, sin, row(mla_qnorm_g[layer]), row(mla_kvnorm_g[layer]),
                            wq[layer], wqr[layer], wk[layer], wv[layer])
        ml = _mla_attention(q, k, v)
        na = _na_attention(p16, _na_bias_tables(na_rpb[layer]))
        xc, hp, aff = _out_proj(hg, ml, na, xc, mods, w_out_b[layer], row(norm_ffn_g[layer]), w_router[layer])
        delta = _experts(_route(aff), aff, hp, w1_b[layer], w3_b[layer], w2_b[layer])
        prev_mods = mods
    return _final_norm(xc, delta, prev_mods, row(final_norm_g))
```

```python
import functools
import math

import numpy as np
import jax
import jax.numpy as jnp
from jax import lax
from jax.experimental import pallas as pl
from jax.experimental.pallas import tpu as pltpu

D_MODEL = 1024
BATCH = 8
SEQ = 4096
DEPTH = 4

CTX_LEN = 256
GRID_W = 64
EPS = 1e-6
MIN_FORGET = 1e-20

D_MIX = D_MODEL
HG_HEADS = 4
HG_DIM = 64
HG_WIDTH = HG_HEADS * HG_DIM
HG_CHUNK = 64
MLA_HEADS = 6
MLA_NOPE = 64
MLA_ROPE = 32
MLA_V = 64
MLA_Q_RANK = 256
MLA_KV_RANK = 128
MLA_WIDTH = MLA_HEADS * MLA_V
NA_HEADS = 6
NA_DIM = 64
NA_WIDTH = NA_HEADS * NA_DIM
NA_WIN_R = 8
NA_WIN_C = 16
HG_IN = 5 * HG_WIDTH
MLA_IN = MLA_Q_RANK + MLA_KV_RANK + MLA_ROPE
NA_IN = 3 * NA_WIDTH
D_IN = HG_IN + MLA_IN + NA_IN
ROPE_BASE = 10000.0
N_EXPERTS = 16
EC_CAPACITY = 2
F_EXPERT = 1024

LANES = 128
SUBLANES = 8
VMEM_LIMIT_BYTES = 56 * 1024 * 1024

TOK = CTX_LEN + SEQ
ROW_TILE = 256
N_ROW_TILES = TOK // ROW_TILE
CTX_TILES = CTX_LEN // ROW_TILE

MLA_HEAD_PAD = LANES
P32_HG = 0
P32_CQ = HG_IN
P32_CKV = P32_CQ + MLA_Q_RANK
P32_KR = P32_CKV + MLA_KV_RANK
P32_KRP = P32_KR + LANES
P32_WIDTH = P32_KRP + LANES
P_WIDTH = P32_WIDTH + NA_IN


def _cparams(n_grid_dims):
    return pltpu.CompilerParams(dimension_semantics=("arbitrary",) * n_grid_dims,
                                vmem_limit_bytes=VMEM_LIMIT_BYTES)


def _dot_nt(a, b):
    return lax.dot_general(a, b, (((1,), (1,)), ((), ())), preferred_element_type=jnp.float32)


def _dot(a, b):
    return jnp.dot(a, b, preferred_element_type=jnp.float32)


def _kind_of_tile(t):
    return (t >= CTX_TILES).astype(jnp.int32)


MOD_SHIFT_A, MOD_SCALE_A, MOD_GATE_A, MOD_SHIFT_F, MOD_SCALE_F, MOD_GATE_F = range(6)


def _mod_row(mod_ref, i):
    return mod_ref[i:i + 1, :]


def _modulated_norm(x, g, shift, scale):
    y = x * lax.rsqrt(jnp.mean(x * x, axis=-1, keepdims=True) + EPS) * g
    return y * (1.0 + scale) + shift


def _in_proj_body(*refs, has_delta):
    if has_delta:
        x_ref, delta_ref, pmod_ref, g_ref, mod_ref, w_ref, xo_ref, p32_ref, p16_ref = refs
        x = x_ref[...] + _mod_row(pmod_ref, MOD_GATE_F) * delta_ref[...]
        xo_ref[...] = x
    else:
        x_ref, g_ref, mod_ref, w_ref, p32_ref, p16_ref = refs
        x = x_ref[...]
    h = _modulated_norm(x, g_ref[...], _mod_row(mod_ref, MOD_SHIFT_A), _mod_row(mod_ref, MOD_SCALE_A))
    p = _dot(h.astype(jnp.bfloat16), w_ref[...])
    p32_ref[...] = p[:, :P32_WIDTH]
    p16_ref[...] = p[:, P32_WIDTH:].astype(p16_ref.dtype)


def _row_spec(width):
    return pl.BlockSpec((None, ROW_TILE, width), lambda b, t: (b, t, 0))


def _mod_spec():
    return pl.BlockSpec((None, None, 6, D_MODEL), lambda b, t: (b, _kind_of_tile(t), 0, 0))


def _in_proj(xc, delta, prev_mods, g, mods, w):
    B = xc.shape[0]
    has_delta = delta is not None
    fixed = [pl.BlockSpec((1, D_MODEL), lambda b, t: (0, 0)), _mod_spec(),
             pl.BlockSpec((D_MODEL, P_WIDTH), lambda b, t: (0, 0))]
    outs = [_row_spec(P32_WIDTH), _row_spec(NA_IN)]
    shapes = [jax.ShapeDtypeStruct((B, TOK, P32_WIDTH), jnp.float32),
              jax.ShapeDtypeStruct((B, TOK, NA_IN), jnp.bfloat16)]
    if has_delta:
        in_specs = [_row_spec(D_MODEL), _row_spec(D_MODEL), _mod_spec()] + fixed
        args = (xc, delta, prev_mods, g, mods, w)
        outs = [_row_spec(D_MODEL)] + outs
        shapes = [jax.ShapeDtypeStruct((B, TOK, D_MODEL), jnp.float32)] + shapes
    else:
        in_specs = [_row_spec(D_MODEL)] + fixed
        args = (xc, g, mods, w)
    res = pl.pallas_call(
        functools.partial(_in_proj_body, has_delta=has_delta),
        grid=(B, N_ROW_TILES),
        in_specs=in_specs,
        out_specs=outs,
        out_shape=shapes,
        compiler_params=_cparams(2),
        name="in_proj",
    )(*args)
    return res if has_delta else [xc] + list(res)


def _rope_partner():
    half = MLA_ROPE // 2
    quarter = half // 2
    partner, sign = [], []
    for r in range(MLA_ROPE):
        base, j = (r // half) * half, r % half
        partner.append(base + (j + quarter if j < quarter else j - quarter))
        sign.append(-1.0 if j < quarter else 1.0)
    return np.array(partner), np.array(sign, np.float32)


def _widen_w_in(w_in):
    L = w_in.shape[0]
    a0, a1 = HG_IN, HG_IN + MLA_IN
    partner, sign = _rope_partner()
    kr = w_in[..., a0 + MLA_Q_RANK + MLA_KV_RANK:a1]
    zl = jnp.zeros((L, D_MODEL, MLA_NOPE), w_in.dtype)
    zr = jnp.zeros((L, D_MODEL, MLA_HEAD_PAD - MLA_NOPE - MLA_ROPE), w_in.dtype)
    na = w_in[..., a1:]
    return jnp.concatenate([
        w_in[..., :a0 + MLA_Q_RANK + MLA_KV_RANK],
        zl, kr, zr,
        zl, kr[..., partner] * sign, zr,
        na[..., :NA_WIDTH] * NA_DIM ** -0.5, na[..., NA_WIDTH:]], axis=-1).astype(jnp.bfloat16)


GRID_H = SEQ // GRID_W
NA_QROWS = 2
NA_QB = NA_QROWS * GRID_W
NA_KROWS = NA_QROWS + NA_WIN_R - 1
NA_KB = NA_KROWS * GRID_W
NA_NBLK = GRID_H // NA_QROWS
NA_MASKED = -1e30


def _na_window_start(m):
    return min(max(NA_QROWS * m - NA_WIN_R // 2, 0), GRID_H - NA_KROWS)


NA_NDROW = 2 * NA_WIN_R - 1
NA_NDCOL = 2 * NA_WIN_C - 1


@functools.lru_cache(maxsize=None)
def _na_patterns():
    pats, pat_of_block = [], []
    for m in range(NA_NBLK):
        pat = np.full((NA_QROWS, NA_KROWS), NA_NDROW, np.int64)
        for qr in range(NA_QROWS):
            r = NA_QROWS * m + qr
            rs = min(max(r - NA_WIN_R // 2, 0), GRID_H - NA_WIN_R)
            for j in range(NA_KROWS):
                kr = _na_window_start(m) + j
                if rs <= kr < rs + NA_WIN_R:
                    pat[qr, j] = kr - r + NA_WIN_R - 1
        for p, other in enumerate(pats):
            if (other == pat).all():
                pat_of_block.append(p)
                break
        else:
            pat_of_block.append(len(pats))
            pats.append(pat)
    return np.stack(pats), tuple(pat_of_block)


def _na_bias_tables(rpb):
    pats, _ = _na_patterns()
    c = np.arange(GRID_W)[:, None]
    kc = np.arange(GRID_W)[None, :]
    cs = np.clip(c - NA_WIN_C // 2, 0, GRID_W - NA_WIN_C)
    col_valid = (kc >= cs) & (kc < cs + NA_WIN_C)
    onehot = (np.arange(NA_NDCOL)[:, None, None] == (kc - c + NA_WIN_C - 1)[None]).astype(np.float32)
    tiles = jnp.einsum('hdj,jck->hdck', rpb.astype(jnp.float32), onehot, precision=lax.Precision.HIGHEST)
    tiles = jnp.where(col_valid, tiles, NA_MASKED)
    tiles = jnp.concatenate([tiles, jnp.full((NA_HEADS, 1, GRID_W, GRID_W), NA_MASKED, jnp.float32)], axis=1)
    rows = [jnp.concatenate([tiles[:, int(d)] for d in pat_row], axis=-1) for pat in pats for pat_row in pat]
    return jnp.stack(rows, axis=1).reshape(NA_HEADS, len(pats), NA_QB, NA_KB)


def _na_body(q_ref, k_ref, v_ref, bias_ref, o_ref):
    _, pat_of_block = _na_patterns()
    lane = lax.broadcasted_iota(jnp.int32, (1, LANES), 1)
    head_masks = [(lane >= NA_DIM * hh) & (lane < NA_DIM * (hh + 1)) for hh in range(2)]
    kc = k_ref[0:CTX_LEN, :]
    vc = v_ref[0:CTX_LEN, :]
    vc_h = [jnp.where(hm, vc, jnp.zeros_like(vc)) for hm in head_masks]

    def block(m, ws, pat):
        qrows = pl.ds(pl.multiple_of(CTX_LEN + m * NA_QB, NA_QB), NA_QB)
        q = q_ref[qrows, :]
        krows = pl.ds(pl.multiple_of(CTX_LEN + ws * GRID_W, GRID_W), NA_KB)
        kwin = k_ref[krows, :]
        vwin = v_ref[krows, :]
        out = None
        for hh in range(2):
            qh = jnp.where(head_masks[hh], q, jnp.zeros_like(q))
            s_w = _dot_nt(qh, kwin) + bias_ref[hh, pat]
            s_c = _dot_nt(qh, kc)
            mx = jnp.maximum(jnp.max(s_w, axis=-1, keepdims=True), jnp.max(s_c, axis=-1, keepdims=True))
            p_w = jnp.exp(s_w - mx)
            p_c = jnp.exp(s_c - mx)
            den = jnp.sum(p_w, axis=-1, keepdims=True) + jnp.sum(p_c, axis=-1, keepdims=True)
            vh = jnp.where(head_masks[hh], vwin, jnp.zeros_like(vwin))
            o = _dot(p_w.astype(jnp.bfloat16), vh) + _dot(p_c.astype(jnp.bfloat16), vc_h[hh])
            o = o / den
            out = o if out is None else out + o
        o_ref[qrows, :] = out.astype(o_ref.dtype)

    interior = [m for m in range(NA_NBLK) if pat_of_block.count(pat_of_block[m]) > 1]
    lo, hi = interior[0], interior[-1] + 1
    assert interior == list(range(lo, hi)) and len({pat_of_block[m] for m in interior}) == 1
    for m in list(range(lo)) + list(range(hi, NA_NBLK)):
        block(m, _na_window_start(m), pat_of_block[m])

    def loop_body(m, carry):
        block(m, NA_QROWS * m - NA_WIN_R // 2, pat_of_block[lo])
        return carry

    lax.fori_loop(lo, hi, loop_body, 0, unroll=2)

    qc = q_ref[0:CTX_LEN, :]
    out = None
    for hh in range(2):
        qh = jnp.where(head_masks[hh], qc, jnp.zeros_like(qc))
        s = _dot_nt(qh, kc)
        p = jnp.exp(s - jnp.max(s, axis=-1, keepdims=True))
        o = _dot(p.astype(jnp.bfloat16), vc_h[hh]) / jnp.sum(p, axis=-1, keepdims=True)
        out = o if out is None else out + o
    o_ref[0:CTX_LEN, :] = out.astype(o_ref.dtype)


def _na_attention(p16, bias):
    B = p16.shape[0]
    n_pat = bias.shape[1]
    n_pairs = NA_WIDTH // LANES

    def col(group):
        return pl.BlockSpec((None, TOK, LANES), lambda b, hp, group=group: (b, 0, group * n_pairs + hp))

    return pl.pallas_call(
        _na_body,
        grid=(B, n_pairs),
        in_specs=[col(0), col(1), col(2),
                  pl.BlockSpec((2, n_pat, NA_QB, NA_KB), lambda b, hp: (hp, 0, 0, 0))],
        out_specs=col(0),
        out_shape=jax.ShapeDtypeStruct((B, TOK, NA_WIDTH), jnp.bfloat16),
        compiler_params=_cparams(2),
        name="na_attention",
    )(p16, p16, p16, bias)


HG_NCHUNK = TOK // HG_CHUNK
HG_CTX_CHUNKS = CTX_LEN // HG_CHUNK


def _split3_bf16(x):
    hi = x.astype(jnp.bfloat16)
    r1 = x - hi.astype(jnp.float32)
    mid = r1.astype(jnp.bfloat16)
    lo = (r1 - mid.astype(jnp.float32)).astype(jnp.bfloat16)
    return hi, mid, lo


HG_CHUNK_SCRATCH = 5


def _hgrn_body(q_ref, i_ref, zf_ref, zb_ref, g_ref, lb_ref, gn_ref, o_ref, of_s, ob_s, *chunk_scratch):
    scratch = (chunk_scratch[:HG_CHUNK_SCRATCH], chunk_scratch[HG_CHUNK_SCRATCH:])
    C = HG_CHUNK
    r_i = lax.broadcasted_iota(jnp.int32, (C, C), 0)
    c_i = lax.broadcasted_iota(jnp.int32, (C, C), 1)
    tri = {True: (c_i <= r_i).astype(jnp.bfloat16), False: (c_i >= r_i).astype(jnp.bfloat16)}
    r2 = lax.broadcasted_iota(jnp.int32, (LANES, LANES), 0) // HG_DIM
    c2 = lax.broadcasted_iota(jnp.int32, (LANES, LANES), 1) // HG_DIM
    same_head = r2 == c2
    half_ones = same_head.astype(jnp.bfloat16)
    row_id = lax.broadcasted_iota(jnp.int32, (C, LANES), 0)
    lane_lo = lax.broadcasted_iota(jnp.int32, (1, LANES), 1) < HG_DIM

    def chunk(c_idx, s_t, fwd, bufs):
        b_s, k_s, q_s, v_s, oi_s = bufs
        rows = pl.ds(pl.multiple_of(c_idx * C, C), C)
        lb = lb_ref[0:1, :] if fwd else lb_ref[1:2, :]
        z = (zf_ref if fwd else zb_ref)[rows, :]
        f = jnp.maximum(lb + (1.0 - lb) * jax.nn.sigmoid(z), MIN_FORGET)
        lf = jnp.log(f)
        hi, mid, lo = _split3_bf16(lf)
        b = _dot(tri[fwd], hi) + _dot(tri[fwd], mid) + _dot(tri[fwd], lo)
        q = q_ref[rows, :] * (HG_DIM ** -0.5)
        v = i_ref[rows, :]
        b_s[...] = b
        k_s[...] = 1.0 - f
        q_s[...] = q
        v_s[...] = v
        b_last = b[C - 1:C, :] if fwd else b[0:1, :]
        k = 1.0 - f
        G = SUBLANES
        n_groups = C // G
        ref = (lambda g: G * g + G - 1) if fwd else (lambda g: G * g)
        later = (lambda g: slice(G * (g + 1), C)) if fwd else (lambda g: slice(0, G * g))

        tiles = []
        for i in range(C):
            grp = slice(i // G * G, i // G * G + G)
            e = jnp.exp(jnp.minimum(b_s[i:i + 1, :] - b_s[grp, :], 0.0))
            visible = (row_id[0:G, :] <= i % G) if fwd else (row_id[0:G, :] >= i % G)
            tiles.append(jnp.where(visible, e * k_s[grp, :] * q_s[i:i + 1, :], 0.0))
        a_same = _dot(jnp.concatenate(tiles, axis=0).astype(jnp.bfloat16), half_ones)
        for i in range(C):
            grp = slice(i // G * G, i // G * G + G)
            oi_s[i:i + 1, :] = jnp.sum(a_same[G * i:G * i + G, :] * v_s[grp, :], axis=0, keepdims=True)

        b_ref_rows = jnp.concatenate([jnp.broadcast_to(b[ref(g):ref(g) + 1, :], (G, LANES))
                                      for g in range(n_groups)], axis=0)
        kp_t = (k * jnp.exp(b_ref_rows - b)).T
        rhs = jnp.where(same_head, jnp.concatenate([kp_t, kp_t], axis=1), 0.0)
        key_groups = [g for g in range(n_groups) if later(g).start < later(g).stop]
        lhs = jnp.concatenate([q[later(g), :] * jnp.exp(b[later(g), :] - b[ref(g):ref(g) + 1, :])
                               for g in key_groups], axis=0)
        res = _dot(lhs.astype(jnp.bfloat16), rhs.astype(jnp.bfloat16))
        lane_group = (lax.broadcasted_iota(jnp.int32, (G, LANES), 1) % HG_DIM) // G
        offs = np.cumsum([0] + [later(g).stop - later(g).start for g in key_groups])
        a_rows = []
        for blk in range(n_groups):
            acc = jnp.zeros((G, LANES), jnp.float32)
            for n, g in enumerate(key_groups):
                if later(g).start <= G * blk < later(g).stop:
                    r0 = int(offs[n]) + G * blk - later(g).start
                    acc = jnp.where(lane_group == g, res[r0:r0 + G, :], acc)
            a_rows.append(acc)
        a_cross = jnp.concatenate(a_rows, axis=0).astype(jnp.bfloat16)
        v_bd = jnp.where(same_head, jnp.concatenate([v, v], axis=0), 0.0).astype(jnp.bfloat16)

        qe = (q * jnp.exp(b)).astype(jnp.bfloat16)
        o = oi_s[...] + _dot(a_cross, v_bd) + _dot_nt(qe, s_t.astype(jnp.bfloat16))
        ke = (k_s[...] * jnp.exp(b_last - b)).astype(jnp.bfloat16)
        upd = _dot(v.T.astype(jnp.bfloat16), ke)
        s_t = s_t * jnp.exp(b_last) + jnp.where(same_head, upd, 0.0)
        return rows, o, s_t

    def scan_body(n, states):
        rows, o, s_fwd = chunk(n, states[0], True, scratch[0])
        of_s[rows, :] = o
        c_idx = jnp.where(n < HG_CTX_CHUNKS, HG_CTX_CHUNKS - 1 - n, HG_NCHUNK + HG_CTX_CHUNKS - 1 - n)
        rows, o, s_bwd = chunk(c_idx, states[1], False, scratch[1])
        ob_s[rows, :] = o
        return s_fwd, s_bwd

    def readout_body(n, carry):
        rows = pl.ds(pl.multiple_of(n * C, C), C)
        o = of_s[rows, :] + ob_s[rows, :]
        sq = o * o
        ms = jnp.where(lane_lo,
                       jnp.sum(jnp.where(lane_lo, sq, 0.0), axis=-1, keepdims=True),
                       jnp.sum(jnp.where(lane_lo, 0.0, sq), axis=-1, keepdims=True)) * (1.0 / HG_DIM)
        y = o * lax.rsqrt(ms + EPS) * gn_ref[...]
        g = g_ref[rows, :]
        o_ref[rows, :] = (y * (g * jax.nn.sigmoid(g))).astype(o_ref.dtype)
        return carry

    zero = jnp.zeros((LANES, LANES), jnp.float32)
    lax.fori_loop(0, HG_NCHUNK, scan_body, (zero, zero), unroll=2)
    lax.fori_loop(0, HG_NCHUNK, readout_body, 0)


def _hgrn_mixer(p32, lb, onorm_g):
    B = p32.shape[0]
    n_pairs = HG_WIDTH // LANES

    def col(group):
        return pl.BlockSpec((None, TOK, LANES), lambda b, hp, group=group: (b, 0, group * n_pairs + hp))

    gn = jnp.tile(onorm_g.astype(jnp.float32), LANES // HG_DIM).reshape(1, LANES)
    chunk_scratch = pltpu.VMEM((HG_CHUNK, LANES), jnp.float32)
    return pl.pallas_call(
        _hgrn_body,
        grid=(B, n_pairs),
        in_specs=[col(0), col(1), col(2), col(3), col(4),
                  pl.BlockSpec((2, LANES), lambda b, hp: (0, hp)),
                  pl.BlockSpec((1, LANES), lambda b, hp: (0, 0))],
        out_specs=pl.BlockSpec((None, TOK, LANES), lambda b, hp: (b, 0, hp)),
        out_shape=jax.ShapeDtypeStruct((B, TOK, HG_WIDTH), jnp.bfloat16),
        scratch_shapes=[pltpu.VMEM((TOK, LANES), jnp.float32)] * 2 + [chunk_scratch] * (2 * HG_CHUNK_SCRATCH),
        compiler_params=_cparams(2),
        name="hgrn2",
    )(p32, p32, p32, p32, p32, lb.astype(jnp.float32), gn)


MLA_QK_WIDTH = MLA_HEADS * MLA_HEAD_PAD
MLA_QSCALE = (MLA_NOPE + MLA_ROPE) ** -0.5 * math.log2(math.e)
MLA_TQ = 256
MLA_KEY_SLAB = 128


def _mla_prep_body(cq_ref, ckv_ref, kr_ref, krp_ref, cos_ref, sin_ref, gq_ref, gkv_ref,
                   wq_ref, wqr_ref, wk_ref, wv_ref, q_out, k_out, v_out):
    def rms(x, g):
        return (x * lax.rsqrt(jnp.mean(x * x, axis=-1, keepdims=True) + EPS) * g).astype(jnp.bfloat16)

    cos = cos_ref[...]
    sin = sin_ref[...]
    cqn = rms(cq_ref[...], gq_ref[...])
    q = _dot(cqn, wq_ref[...])
    qr = _dot(cqn, wqr_ref[...])
    ckvn = rms(ckv_ref[...], gkv_ref[...])
    kn = _dot(ckvn, wk_ref[...])
    v_out[...] = _dot(ckvn, wv_ref[...]).T.astype(v_out.dtype)
    k_rope = kr_ref[...] * cos + krp_ref[...] * sin
    for h in range(MLA_HEADS):
        sl = slice(h * MLA_HEAD_PAD, (h + 1) * MLA_HEAD_PAD)
        q_out[:, sl] = ((q[:, sl] * cos + qr[:, sl] * sin) * MLA_QSCALE).astype(q_out.dtype)
        k_out[:, sl] = (kn[:, sl] + k_rope).astype(k_out.dtype)


def _mla_weights(w_uq, w_ukv):
    partner, sign = _rope_partner()
    L = w_uq.shape[0]
    wq = w_uq.reshape(L, MLA_Q_RANK, MLA_HEADS, MLA_NOPE + MLA_ROPE)
    q_nope, q_rope = wq[..., :MLA_NOPE], wq[..., MLA_NOPE:]
    pad = jnp.zeros(wq.shape[:3] + (MLA_HEAD_PAD - MLA_NOPE - MLA_ROPE,), w_uq.dtype)
    wq_pad = jnp.concatenate([q_nope, q_rope, pad], axis=-1)
    wq_rot = jnp.concatenate([jnp.zeros_like(q_nope), q_rope[..., partner] * sign, pad], axis=-1)
    wkv = w_ukv.reshape(L, MLA_KV_RANK, MLA_HEADS, MLA_NOPE + MLA_V)
    k_nope, v = wkv[..., :MLA_NOPE], wkv[..., MLA_NOPE:]
    wk_pad = jnp.concatenate([k_nope, jnp.zeros(k_nope.shape[:3] + (MLA_HEAD_PAD - MLA_NOPE,), w_ukv.dtype)], axis=-1)
    bf = lambda a, n: a.reshape(L, a.shape[1], n).astype(jnp.bfloat16)
    return bf(wq_pad, MLA_QK_WIDTH), bf(wq_rot, MLA_QK_WIDTH), bf(wk_pad, MLA_QK_WIDTH), bf(v, MLA_WIDTH)


def _rope_tables():
    t = jnp.arange(SEQ)
    half = MLA_ROPE // 2
    inv = ROPE_BASE ** (-jnp.arange(0, half, 2, dtype=jnp.float32) / half)
    ang_r = (t // GRID_W).astype(jnp.float32)[:, None] * inv
    ang_c = (t % GRID_W).astype(jnp.float32)[:, None] * inv
    ang = jnp.concatenate([ang_r, ang_r, ang_c, ang_c], axis=-1)
    ones = jnp.ones((SEQ, MLA_NOPE), jnp.float32)
    zeros_n = jnp.zeros((SEQ, MLA_NOPE), jnp.float32)
    zeros_p = jnp.zeros((SEQ, MLA_HEAD_PAD - MLA_NOPE - MLA_ROPE), jnp.float32)
    cos_l = jnp.concatenate([ones, jnp.cos(ang), zeros_p], axis=-1)
    sin_l = jnp.concatenate([zeros_n, jnp.sin(ang), zeros_p], axis=-1)
    lane = np.arange(MLA_HEAD_PAD)
    cos_c = jnp.broadcast_to(jnp.asarray((lane < MLA_NOPE + MLA_ROPE).astype(np.float32)), (CTX_LEN, MLA_HEAD_PAD))
    sin_c = jnp.zeros((CTX_LEN, MLA_HEAD_PAD), jnp.float32)
    return jnp.concatenate([cos_c, cos_l], axis=0), jnp.concatenate([sin_c, sin_l], axis=0)


def _mla_prep(p32, cos, sin, gq, gkv, wq, wqr, wk, wv):
    B = p32.shape[0]
    full = lambda a: pl.BlockSpec(a.shape, lambda b, t: (0,) * a.ndim)
    rows = lambda width, blk: pl.BlockSpec((None, ROW_TILE, width), lambda b, t, blk=blk: (b, t, blk))
    tab = pl.BlockSpec((ROW_TILE, MLA_HEAD_PAD), lambda b, t: (t, 0))
    return pl.pallas_call(
        _mla_prep_body,
        grid=(B, N_ROW_TILES),
        in_specs=[rows(MLA_Q_RANK, P32_CQ // MLA_Q_RANK), rows(MLA_KV_RANK, P32_CKV // MLA_KV_RANK),
                  rows(LANES, P32_KR // LANES), rows(LANES, P32_KRP // LANES), tab, tab,
                  full(gq), full(gkv), full(wq), full(wqr), full(wk), full(wv)],
        out_specs=[rows(MLA_QK_WIDTH, 0), rows(MLA_QK_WIDTH, 0),
                   pl.BlockSpec((None, MLA_WIDTH, ROW_TILE), lambda b, t: (b, 0, t))],
        out_shape=[jax.ShapeDtypeStruct((B, TOK, MLA_QK_WIDTH), jnp.bfloat16),
                   jax.ShapeDtypeStruct((B, TOK, MLA_QK_WIDTH), jnp.bfloat16),
                   jax.ShapeDtypeStruct((B, MLA_WIDTH, TOK), jnp.bfloat16)],
        compiler_params=_cparams(2),
        name="mla_prep",
    )(p32, p32, p32, p32, cos, sin, gq, gkv, wq, wqr, wk, wv)


def _mla_attn_body(q_ref, k_ref, vt_ref, o_ref, sa_s, sb_s, ma_s, mb_s):
    heads = [slice(hh * MLA_HEAD_PAD, (hh + 1) * MLA_HEAD_PAD) for hh in range(2)]

    def over_keys(x, op):
        slabs = op(x.reshape(x.shape[0] // MLA_KEY_SLAB, MLA_KEY_SLAB, x.shape[1]), axis=0)
        return op(slabs, axis=0, keepdims=True)

    def qrows_of(j):
        return pl.ds(pl.multiple_of(CTX_LEN + j * MLA_TQ, MLA_TQ), MLA_TQ)

    def scores(j, s_s, m_s):
        for hh in range(2):
            st = _dot_nt(k_ref[:, heads[hh]], q_ref[qrows_of(j), heads[hh]])
            s_s[hh] = st
            m_s[hh, 0:1, :] = over_keys(st, jnp.max)

    def finish(st_of, m_of, qrows, n_keys):
        outs = []
        for hh in range(2):
            p = jnp.exp2(st_of(hh) - m_of(hh))
            den = over_keys(p, jnp.sum)
            ot = _dot(vt_ref[hh * MLA_V:(hh + 1) * MLA_V, 0:n_keys], p.astype(jnp.bfloat16))
            outs.append(ot / den)
        o_ref[qrows, :] = jnp.concatenate(outs, axis=0).T.astype(o_ref.dtype)

    def finish_block(j, s_s, m_s):
        finish(lambda hh: s_s[hh], lambda hh: m_s[hh, 0:1, :], qrows_of(j), TOK)

    n_blocks = SEQ // MLA_TQ
    scores(0, sa_s, ma_s)

    def pair_body(t, carry):
        finish_block(2 * t, sa_s, ma_s)
        scores(2 * t + 1, sb_s, mb_s)
        finish_block(2 * t + 1, sb_s, mb_s)
        scores(jnp.minimum(2 * t + 2, n_blocks - 1), sa_s, ma_s)
        return carry

    lax.fori_loop(0, n_blocks // 2, pair_body, 0)

    ctx_st = [_dot_nt(k_ref[0:CTX_LEN, heads[hh]], q_ref[0:CTX_LEN, heads[hh]]) for hh in range(2)]
    finish(lambda hh: ctx_st[hh], lambda hh: over_keys(ctx_st[hh], jnp.max), slice(0, CTX_LEN), CTX_LEN)


def _mla_attention(q, k, v):
    B = q.shape[0]
    qk = pl.BlockSpec((None, TOK, 2 * MLA_HEAD_PAD), lambda b, hp: (b, 0, hp))
    return pl.pallas_call(
        _mla_attn_body,
        grid=(B, MLA_WIDTH // LANES),
        in_specs=[qk, qk, pl.BlockSpec((None, LANES, TOK), lambda b, hp: (b, hp, 0))],
        out_specs=pl.BlockSpec((None, TOK, LANES), lambda b, hp: (b, 0, hp)),
        out_shape=jax.ShapeDtypeStruct((B, TOK, MLA_WIDTH), jnp.bfloat16),
        scratch_shapes=[pltpu.VMEM((2, TOK, MLA_TQ), jnp.float32)] * 2
        + [pltpu.VMEM((2, SUBLANES, MLA_TQ), jnp.float32)] * 2,
        compiler_params=_cparams(2),
        name="mla_attention",
    )(q, k, v)


def _out_proj_body(hg_ref, ml_ref, na_ref, x_ref, mod_ref, w_ref, gf_ref, wr_ref, o_ref, h_ref, aff_ref):
    o = (_dot(hg_ref[...], w_ref[0:HG_WIDTH, :])
         + _dot(ml_ref[...], w_ref[HG_WIDTH:HG_WIDTH + MLA_WIDTH, :])
         + _dot(na_ref[...], w_ref[HG_WIDTH + MLA_WIDTH:, :]))
    x = x_ref[...] + _mod_row(mod_ref, MOD_GATE_A) * o
    o_ref[...] = x
    h = _modulated_norm(x, gf_ref[...], _mod_row(mod_ref, MOD_SHIFT_F), _mod_row(mod_ref, MOD_SCALE_F))
    h_ref[...] = h
    logits = _dot(h.astype(jnp.bfloat16), wr_ref[...])
    lane = lax.broadcasted_iota(jnp.int32, logits.shape, 1)
    logits = jnp.where(lane < N_EXPERTS, logits, NA_MASKED)
    e = jnp.exp(logits - jnp.max(logits, axis=-1, keepdims=True))
    aff = e / jnp.sum(e, axis=-1, keepdims=True)
    aff_ref[...] = aff.T[0:N_EXPERTS, :]


def _out_proj(hg, ml, na, xc, mods, w, g_ffn, w_router):
    B = xc.shape[0]
    return pl.pallas_call(
        _out_proj_body,
        grid=(B, N_ROW_TILES),
        in_specs=[_row_spec(HG_WIDTH), _row_spec(MLA_WIDTH), _row_spec(NA_WIDTH), _row_spec(D_MODEL),
                  _mod_spec(),
                  pl.BlockSpec((D_MIX, D_MODEL), lambda b, t: (0, 0)),
                  pl.BlockSpec((1, D_MODEL), lambda b, t: (0, 0)),
                  pl.BlockSpec((D_MODEL, LANES), lambda b, t: (0, 0))],
        out_specs=[_row_spec(D_MODEL), _row_spec(D_MODEL),
                   pl.BlockSpec((None, N_EXPERTS, ROW_TILE), lambda b, t: (b, 0, t))],
        out_shape=[jax.ShapeDtypeStruct((B, TOK, D_MODEL), jnp.float32),
                   jax.ShapeDtypeStruct((B, TOK, D_MODEL), jnp.float32),
                   jax.ShapeDtypeStruct((B, N_EXPERTS, TOK), jnp.float32)],
        compiler_params=_cparams(2),
        name="out_proj",
    )(hg, ml, na, xc, mods, w, g_ffn, w_router)


CAP_CTX = EC_CAPACITY * CTX_LEN // N_EXPERTS
CAP_LAT = EC_CAPACITY * SEQ // N_EXPERTS
CAP = CAP_CTX + CAP_LAT
IDX_PAD = -(-CAP // LANES) * LANES
ONE_BITS = 0x3F800000


def _prefix_count(m, upper):
    outs, off = [], jnp.zeros((m.shape[0], 1), jnp.float32)
    for j in range(m.shape[1] // LANES):
        loc = _dot(m[:, j * LANES:(j + 1) * LANES].astype(jnp.bfloat16), upper)
        outs.append(loc + off)
        off = off + loc[:, LANES - 1:LANES]
    return jnp.concatenate(outs, axis=1)


def _select_top(a, k, upper):
    rows = a.shape[0]

    def step(_, lohi):
        lo, hi = lohi
        mid = lo + ((hi - lo + 1) >> 1)
        cnt = jnp.sum((a >= pltpu.bitcast(mid, jnp.float32)).astype(jnp.int32), axis=-1, keepdims=True)
        ok = cnt >= k
        return jnp.where(ok, mid, lo), jnp.where(ok, hi, mid - 1)

    thr, _ = lax.fori_loop(0, 31, step, (jnp.zeros((rows, 1), jnp.int32), jnp.full((rows, 1), ONE_BITS, jnp.int32)))
    thr = pltpu.bitcast(thr, jnp.float32)
    gt = (a > thr).astype(jnp.float32)
    eq = (a == thr).astype(jnp.float32)
    need = k - jnp.sum(gt, axis=-1, keepdims=True)
    return gt + eq * (_prefix_count(eq, upper) <= need).astype(jnp.float32)


def _route_body(aff_ref, idx_ref):
    a = aff_ref[...]
    r = lax.broadcasted_iota(jnp.int32, (LANES, LANES), 0)
    c = lax.broadcasted_iota(jnp.int32, (LANES, LANES), 1)
    upper = (r <= c).astype(jnp.bfloat16)
    sel = jnp.concatenate([_select_top(a[:, :CTX_LEN], CAP_CTX, upper),
                           _select_top(a[:, CTX_LEN:], CAP_LAT, upper)], axis=1)
    pos = _prefix_count(sel, upper)
    lane = lax.broadcasted_iota(jnp.int32, (LANES, LANES), 1)
    for sb in range(IDX_PAD // LANES):
        slot = (lax.broadcasted_iota(jnp.int32, (LANES, 1), 0) + sb * LANES).astype(jnp.float32)
        cols = jnp.zeros((LANES, LANES), jnp.float32)
        for e in range(N_EXPERTS):
            cnt = jnp.sum((pos[e:e + 1, :] <= slot).astype(jnp.float32), axis=-1, keepdims=True)
            cols = jnp.where(lane == e, cnt, cols)
        idx = jnp.minimum(cols.T[0:N_EXPERTS, :], TOK - 1)
        idx_ref[:, sb * LANES:(sb + 1) * LANES] = idx.astype(jnp.int32)


def _route(aff):
    B = aff.shape[0]
    return pl.pallas_call(
        _route_body,
        grid=(B,),
        in_specs=[pl.BlockSpec((None, N_EXPERTS, TOK), lambda b: (b, 0, 0))],
        out_specs=pl.BlockSpec((None, N_EXPERTS, IDX_PAD), lambda b: (b, 0, 0)),
        out_shape=jax.ShapeDtypeStruct((B, N_EXPERTS, IDX_PAD), jnp.int32),
        compiler_params=_cparams(1),
        name="route",
    )(aff)


SCATTER_BATCH = 4


def _smem_row(n):
    return pl.BlockSpec((None, None, 1, n), lambda b, e: (b, e, 0, 0), memory_space=pltpu.SMEM)


def _slot_spec():
    return pl.BlockSpec((None, None, CAP, D_MODEL), lambda b, e: (b, e, 0, 0))


def _gather_body(idx_ref, h_ref, xs_ref, xs_s):
    for s in range(CAP):
        xs_s[s:s + 1, :] = h_ref[pl.ds(idx_ref[0, s], 1), :]
    xs_ref[...] = xs_s[...].astype(xs_ref.dtype)


def _swiglu_body(xs_ref, w1_ref, w3_ref, w2_ref, y_ref):
    xs = xs_ref[...]
    a = _dot(xs, w1_ref[...])
    u = _dot(xs, w3_ref[...])
    y_ref[...] = _dot((a * jax.nn.sigmoid(a) * u).astype(jnp.bfloat16), w2_ref[...])


def _scatter_body(idx_ref, aff_ref, y_ref, acc_ref):
    @pl.when(pl.program_id(1) == 0)
    def _():
        acc_ref[...] = jnp.zeros_like(acc_ref)

    for s0 in range(0, CAP, SCATTER_BATCH):
        rows = [idx_ref[0, s0 + j] for j in range(SCATTER_BATCH)]
        new = [acc_ref[pl.ds(rows[j], 1), :] + aff_ref[0, rows[j]] * y_ref[s0 + j:s0 + j + 1, :]
               for j in range(SCATTER_BATCH)]
        for j in range(SCATTER_BATCH):
            acc_ref[pl.ds(rows[j], 1), :] = new[j]


def _experts(idx, aff, h, w1, w3, w2):
    B = h.shape[0]
    grid = (B, N_EXPERTS)
    idx4, aff4 = idx[:, :, None, :], aff[:, :, None, :]
    sample = pl.BlockSpec((None, TOK, D_MODEL), lambda b, e: (b, 0, 0))
    wspec = lambda k, n: pl.BlockSpec((None, k, n), lambda b, e: (e, 0, 0))
    slots = lambda dtype: jax.ShapeDtypeStruct((B, N_EXPERTS, CAP, D_MODEL), dtype)
    xs = pl.pallas_call(
        _gather_body, grid=grid,
        in_specs=[_smem_row(IDX_PAD), sample],
        out_specs=_slot_spec(), out_shape=slots(jnp.bfloat16),
        scratch_shapes=[pltpu.VMEM((CAP, D_MODEL), jnp.float32)],
        compiler_params=_cparams(2), name="expert_gather",
    )(idx4, h)
    y = pl.pallas_call(
        _swiglu_body, grid=grid,
        in_specs=[_slot_spec(), wspec(D_MODEL, F_EXPERT), wspec(D_MODEL, F_EXPERT), wspec(F_EXPERT, D_MODEL)],
        out_specs=_slot_spec(), out_shape=slots(jnp.float32),
        compiler_params=_cparams(2), name="expert_swiglu",
    )(xs, w1, w3, w2)
    return pl.pallas_call(
        _scatter_body, grid=grid,
        in_specs=[_smem_row(IDX_PAD), _smem_row(TOK), _slot_spec()],
        out_specs=sample, out_shape=jax.ShapeDtypeStruct((B, TOK, D_MODEL), jnp.float32),
        compiler_params=_cparams(2), name="expert_scatter",
    )(idx4, aff4, y)


def _final_body(x_ref, delta_ref, mod_ref, g_ref, o_ref):
    x = x_ref[...] + _mod_row(mod_ref, MOD_GATE_F) * delta_ref[...]
    o_ref[...] = x * lax.rsqrt(jnp.mean(x * x, axis=-1, keepdims=True) + EPS) * g_ref[...]


def _final_norm(xc, delta, mods, g):
    B = xc.shape[0]
    lat_rows = pl.BlockSpec((None, ROW_TILE, D_MODEL), lambda b, t: (b, t + CTX_TILES, 0))
    return pl.pallas_call(
        _final_body,
        grid=(B, SEQ // ROW_TILE),
        in_specs=[lat_rows, lat_rows,
                  pl.BlockSpec((None, None, 6, D_MODEL), lambda b, t: (b, 1, 0, 0)),
                  pl.BlockSpec((1, D_MODEL), lambda b, t: (0, 0))],
        out_specs=pl.BlockSpec((None, ROW_TILE, D_MODEL), lambda b, t: (b, t, 0)),
        out_shape=jax.ShapeDtypeStruct((B, SEQ, D_MODEL), jnp.float32),
        compiler_params=_cparams(2),
        name="final_norm",
    )(xc, delta, mods, g)


def kernel(x, c, ctx, c_ctx, ada_w, ada_b, norm_mix_g, norm_ffn_g, w_in, hgrn_lb_logits,
           hgrn_onorm_g, mla_qnorm_g, mla_w_uq, mla_kvnorm_g, mla_w_ukv, na_rpb, w_out,
           router_w, exp_w1, exp_w3, exp_w2, final_norm_g):
    B = x.shape[0]
    lb_w = jax.nn.softmax(hgrn_lb_logits.astype(jnp.float32), axis=1)
    lb_all = jnp.cumsum(lb_w, axis=1) - lb_w[:, :1]
    s_lat = jax.nn.silu(c)
    s_ctx = jax.nn.silu(c_ctx)
    w_in_b = _widen_w_in(w_in)
    w_out_b = w_out.astype(jnp.bfloat16)
    wq, wqr, wk, wv = _mla_weights(mla_w_uq, mla_w_ukv)
    cos, sin = _rope_tables()
    w_router = jnp.pad(router_w, ((0, 0), (0, 0), (0, LANES - N_EXPERTS))).astype(jnp.bfloat16)
    w1_b, w3_b, w2_b = (w.astype(jnp.bfloat16) for w in (exp_w1, exp_w3, exp_w2))
    row = lambda g: g.reshape(1, -1).astype(jnp.float32)
    xc = jnp.concatenate([ctx, x], axis=1)
    delta, prev_mods = None, None
    for layer in range(DEPTH):
        mod_l = (s_lat @ ada_w[layer] + ada_b[layer]).reshape(B, 6, D_MODEL)
        mod_c = jnp.broadcast_to((s_ctx @ ada_w[layer] + ada_b[layer]).reshape(1, 6, D_MODEL), (B, 6, D_MODEL))
        mods = jnp.stack([mod_c, mod_l], axis=1)

        xc, p32, p16 = _in_proj(xc, delta, prev_mods, row(norm_mix_g[layer]), mods, w_in_b[layer])
        hg = _hgrn_mixer(p32, lb_all[:, layer], hgrn_onorm_g[layer])
        q, k, v = _mla_prep(p32, cos, sin, row(mla_qnorm_g[layer]), row(mla_kvnorm_g[layer]),
                            wq[layer], wqr[layer], wk[layer], wv[layer])
        ml = _mla_attention(q, k, v)
        na = _na_attention(p16, _na_bias_tables(na_rpb[layer]))
        xc, h, aff = _out_proj(hg, ml, na, xc, mods, w_out_b[layer], row(norm_ffn_g[layer]), w_router[layer])
        delta = _experts(_route(aff), aff, h, w1_b[layer], w3_b[layer], w2_b[layer])
        prev_mods = mods
    return _final_norm(xc, delta, prev_mods, row(final_norm_g))
```

```python
import functools
import math

import numpy as np
import jax
import jax.numpy as jnp
from jax import lax
from jax.experimental import pallas as pl
from jax.experimental.pallas import tpu as pltpu

D_MODEL = 1024
BATCH = 8
SEQ = 4096
DEPTH = 4

CTX_LEN = 256
GRID_W = 64
EPS = 1e-6
MIN_FORGET = 1e-20

D_MIX = D_MODEL
HG_HEADS = 4
HG_DIM = 64
HG_WIDTH = HG_HEADS * HG_DIM
HG_CHUNK = 64
MLA_HEADS = 6
MLA_NOPE = 64
MLA_ROPE = 32
MLA_V = 64
MLA_Q_RANK = 256
MLA_KV_RANK = 128
MLA_WIDTH = MLA_HEADS * MLA_V
NA_HEADS = 6
NA_DIM = 64
NA_WIDTH = NA_HEADS * NA_DIM
NA_WIN_R = 8
NA_WIN_C = 16
HG_IN = 5 * HG_WIDTH
MLA_IN = MLA_Q_RANK + MLA_KV_RANK + MLA_ROPE
NA_IN = 3 * NA_WIDTH
D_IN = HG_IN + MLA_IN + NA_IN
ROPE_BASE = 10000.0
N_EXPERTS = 16
EC_CAPACITY = 2
F_EXPERT = 1024

LANES = 128
SUBLANES = 8
VMEM_LIMIT_BYTES = 56 * 1024 * 1024

TOK = CTX_LEN + SEQ
ROW_TILE = 256
N_ROW_TILES = TOK // ROW_TILE
CTX_TILES = CTX_LEN // ROW_TILE

MLA_HEAD_PAD = LANES
P32_HG = 0
P32_CQ = HG_IN
P32_CKV = P32_CQ + MLA_Q_RANK
P32_KR = P32_CKV + MLA_KV_RANK
P32_KRP = P32_KR + LANES
P32_WIDTH = P32_KRP + LANES
P_WIDTH = P32_WIDTH + NA_IN


def _cparams(n_grid_dims):
    return pltpu.CompilerParams(dimension_semantics=("arbitrary",) * n_grid_dims,
                                vmem_limit_bytes=VMEM_LIMIT_BYTES)


def _dot_nt(a, b):
    return lax.dot_general(a, b, (((1,), (1,)), ((), ())), preferred_element_type=jnp.float32)


def _dot(a, b):
    return jnp.dot(a, b, preferred_element_type=jnp.float32)


def _kind_of_tile(t):
    return (t >= CTX_TILES).astype(jnp.int32)


MOD_SHIFT_A, MOD_SCALE_A, MOD_GATE_A, MOD_SHIFT_F, MOD_SCALE_F, MOD_GATE_F = range(6)


def _mod_row(mod_ref, i):
    return mod_ref[i:i + 1, :]


def _modulated_norm(x, g, shift, scale):
    y = x * lax.rsqrt(jnp.mean(x * x, axis=-1, keepdims=True) + EPS) * g
    return y * (1.0 + scale) + shift


def _in_proj_body(*refs, has_delta):
    if has_delta:
        x_ref, delta_ref, pmod_ref, g_ref, mod_ref, w_ref, xo_ref, p32_ref, p16_ref = refs
        x = x_ref[...] + _mod_row(pmod_ref, MOD_GATE_F) * delta_ref[...]
        xo_ref[...] = x
    else:
        x_ref, g_ref, mod_ref, w_ref, p32_ref, p16_ref = refs
        x = x_ref[...]
    h = _modulated_norm(x, g_ref[...], _mod_row(mod_ref, MOD_SHIFT_A), _mod_row(mod_ref, MOD_SCALE_A))
    p = _dot(h.astype(jnp.bfloat16), w_ref[...])
    p32_ref[...] = p[:, :P32_WIDTH]
    p16_ref[...] = p[:, P32_WIDTH:].astype(p16_ref.dtype)


def _row_spec(width):
    return pl.BlockSpec((None, ROW_TILE, width), lambda b, t: (b, t, 0))


def _mod_spec():
    return pl.BlockSpec((None, None, 6, D_MODEL), lambda b, t: (b, _kind_of_tile(t), 0, 0))


def _in_proj(xc, delta, prev_mods, g, mods, w):
    B = xc.shape[0]
    has_delta = delta is not None
    fixed = [pl.BlockSpec((1, D_MODEL), lambda b, t: (0, 0)), _mod_spec(),
             pl.BlockSpec((D_MODEL, P_WIDTH), lambda b, t: (0, 0))]
    outs = [_row_spec(P32_WIDTH), _row_spec(NA_IN)]
    shapes = [jax.ShapeDtypeStruct((B, TOK, P32_WIDTH), jnp.float32),
              jax.ShapeDtypeStruct((B, TOK, NA_IN), jnp.bfloat16)]
    if has_delta:
        in_specs = [_row_spec(D_MODEL), _row_spec(D_MODEL), _mod_spec()] + fixed
        args = (xc, delta, prev_mods, g, mods, w)
        outs = [_row_spec(D_MODEL)] + outs
        shapes = [jax.ShapeDtypeStruct((B, TOK, D_MODEL), jnp.float32)] + shapes
    else:
        in_specs = [_row_spec(D_MODEL)] + fixed
        args = (xc, g, mods, w)
    res = pl.pallas_call(
        functools.partial(_in_proj_body, has_delta=has_delta),
        grid=(B, N_ROW_TILES),
        in_specs=in_specs,
        out_specs=outs,
        out_shape=shapes,
        compiler_params=_cparams(2),
        name="in_proj",
    )(*args)
    return res if has_delta else [xc] + list(res)


def _rope_partner():
    half = MLA_ROPE // 2
    quarter = half // 2
    partner, sign = [], []
    for r in range(MLA_ROPE):
        base, j = (r // half) * half, r % half
        partner.append(base + (j + quarter if j < quarter else j - quarter))
        sign.append(-1.0 if j < quarter else 1.0)
    return np.array(partner), np.array(sign, np.float32)


def _widen_w_in(w_in):
    L = w_in.shape[0]
    a0, a1 = HG_IN, HG_IN + MLA_IN
    partner, sign = _rope_partner()
    kr = w_in[..., a0 + MLA_Q_RANK + MLA_KV_RANK:a1]
    zl = jnp.zeros((L, D_MODEL, MLA_NOPE), w_in.dtype)
    zr = jnp.zeros((L, D_MODEL, MLA_HEAD_PAD - MLA_NOPE - MLA_ROPE), w_in.dtype)
    na = w_in[..., a1:]
    return jnp.concatenate([
        w_in[..., :a0 + MLA_Q_RANK + MLA_KV_RANK],
        zl, kr, zr,
        zl, kr[..., partner] * sign, zr,
        na[..., :NA_WIDTH] * NA_DIM ** -0.5, na[..., NA_WIDTH:]], axis=-1).astype(jnp.bfloat16)


GRID_H = SEQ // GRID_W
NA_QROWS = 4
NA_QB = NA_QROWS * GRID_W
NA_KROWS = NA_QROWS + NA_WIN_R - 1
NA_KB = NA_KROWS * GRID_W
NA_NBLK = GRID_H // NA_QROWS
NA_MASKED = -1e30


def _na_window_start(m):
    return min(max(NA_QROWS * m - NA_WIN_R // 2, 0), GRID_H - NA_KROWS)


NA_NDROW = 2 * NA_WIN_R - 1
NA_NDCOL = 2 * NA_WIN_C - 1


@functools.lru_cache(maxsize=None)
def _na_patterns():
    pats, pat_of_block = [], []
    for m in range(NA_NBLK):
        pat = np.full((NA_QROWS, NA_KROWS), NA_NDROW, np.int64)
        for qr in range(NA_QROWS):
            r = NA_QROWS * m + qr
            rs = min(max(r - NA_WIN_R // 2, 0), GRID_H - NA_WIN_R)
            for j in range(NA_KROWS):
                kr = _na_window_start(m) + j
                if rs <= kr < rs + NA_WIN_R:
                    pat[qr, j] = kr - r + NA_WIN_R - 1
        for p, other in enumerate(pats):
            if (other == pat).all():
                pat_of_block.append(p)
                break
        else:
            pat_of_block.append(len(pats))
            pats.append(pat)
    return np.stack(pats), tuple(pat_of_block)


def _na_bias_tables(rpb):
    pats, _ = _na_patterns()
    c = np.arange(GRID_W)[:, None]
    kc = np.arange(GRID_W)[None, :]
    cs = np.clip(c - NA_WIN_C // 2, 0, GRID_W - NA_WIN_C)
    col_valid = (kc >= cs) & (kc < cs + NA_WIN_C)
    onehot = (np.arange(NA_NDCOL)[:, None, None] == (kc - c + NA_WIN_C - 1)[None]).astype(np.float32)
    tiles = jnp.einsum('hdj,jck->hdck', rpb.astype(jnp.float32), onehot, precision=lax.Precision.HIGHEST)
    tiles = jnp.where(col_valid, tiles, NA_MASKED)
    tiles = jnp.concatenate([tiles, jnp.full((NA_HEADS, 1, GRID_W, GRID_W), NA_MASKED, jnp.float32)], axis=1)
    rows = [jnp.concatenate([tiles[:, int(d)] for d in pat_row], axis=-1) for pat in pats for pat_row in pat]
    return jnp.stack(rows, axis=1).reshape(NA_HEADS, len(pats), NA_QB, NA_KB)


def _na_body(q_ref, k_ref, v_ref, bias_ref, o_ref):
    _, pat_of_block = _na_patterns()
    lane = lax.broadcasted_iota(jnp.int32, (1, LANES), 1)
    head_masks = [(lane >= NA_DIM * hh) & (lane < NA_DIM * (hh + 1)) for hh in range(2)]
    kc = k_ref[0:CTX_LEN, :]
    vc = v_ref[0:CTX_LEN, :]
    vc_h = [jnp.where(hm, vc, jnp.zeros_like(vc)) for hm in head_masks]

    def block(m, ws, pat):
        qrows = pl.ds(pl.multiple_of(CTX_LEN + m * NA_QB, NA_QB), NA_QB)
        q = q_ref[qrows, :]
        krows = pl.ds(pl.multiple_of(CTX_LEN + ws * GRID_W, GRID_W), NA_KB)
        kwin = k_ref[krows, :]
        vwin = v_ref[krows, :]
        out = None
        for hh in range(2):
            qh = jnp.where(head_masks[hh], q, jnp.zeros_like(q))
            s_w = _dot_nt(qh, kwin) + bias_ref[hh, pat]
            s_c = _dot_nt(qh, kc)
            mx = jnp.maximum(jnp.max(s_w, axis=-1, keepdims=True), jnp.max(s_c, axis=-1, keepdims=True))
            p_w = jnp.exp(s_w - mx)
            p_c = jnp.exp(s_c - mx)
            den = jnp.sum(p_w, axis=-1, keepdims=True) + jnp.sum(p_c, axis=-1, keepdims=True)
            vh = jnp.where(head_masks[hh], vwin, jnp.zeros_like(vwin))
            o = _dot(p_w.astype(jnp.bfloat16), vh) + _dot(p_c.astype(jnp.bfloat16), vc_h[hh])
            o = o / den
            out = o if out is None else out + o
        o_ref[qrows, :] = out.astype(o_ref.dtype)

    interior = [m for m in range(NA_NBLK) if pat_of_block.count(pat_of_block[m]) > 1]
    lo, hi = interior[0], interior[-1] + 1
    assert interior == list(range(lo, hi)) and len({pat_of_block[m] for m in interior}) == 1
    for m in list(range(lo)) + list(range(hi, NA_NBLK)):
        block(m, _na_window_start(m), pat_of_block[m])

    def loop_body(m, carry):
        block(m, NA_QROWS * m - NA_WIN_R // 2, pat_of_block[lo])
        return carry

    lax.fori_loop(lo, hi, loop_body, 0, unroll=2)

    qc = q_ref[0:CTX_LEN, :]
    out = None
    for hh in range(2):
        qh = jnp.where(head_masks[hh], qc, jnp.zeros_like(qc))
        s = _dot_nt(qh, kc)
        p = jnp.exp(s - jnp.max(s, axis=-1, keepdims=True))
        o = _dot(p.astype(jnp.bfloat16), vc_h[hh]) / jnp.sum(p, axis=-1, keepdims=True)
        out = o if out is None else out + o
    o_ref[0:CTX_LEN, :] = out.astype(o_ref.dtype)


def _na_attention(p16, bias):
    B = p16.shape[0]
    n_pat = bias.shape[1]
    n_pairs = NA_WIDTH // LANES

    def col(group):
        return pl.BlockSpec((None, TOK, LANES), lambda b, hp, group=group: (b, 0, group * n_pairs + hp))

    return pl.pallas_call(
        _na_body,
        grid=(B, n_pairs),
        in_specs=[col(0), col(1), col(2),
                  pl.BlockSpec((2, n_pat, NA_QB, NA_KB), lambda b, hp: (hp, 0, 0, 0))],
        out_specs=col(0),
        out_shape=jax.ShapeDtypeStruct((B, TOK, NA_WIDTH), jnp.bfloat16),
        compiler_params=_cparams(2),
        name="na_attention",
    )(p16, p16, p16, bias)


HG_NCHUNK = TOK // HG_CHUNK
HG_CTX_CHUNKS = CTX_LEN // HG_CHUNK


def _split3_bf16(x):
    hi = x.astype(jnp.bfloat16)
    r1 = x - hi.astype(jnp.float32)
    mid = r1.astype(jnp.bfloat16)
    lo = (r1 - mid.astype(jnp.float32)).astype(jnp.bfloat16)
    return hi, mid, lo


HG_CHUNK_SCRATCH = 5


def _hgrn_body(q_ref, i_ref, zf_ref, zb_ref, g_ref, lb_ref, gn_ref, o_ref, of_s, ob_s, *chunk_scratch):
    scratch = (chunk_scratch[:HG_CHUNK_SCRATCH], chunk_scratch[HG_CHUNK_SCRATCH:])
    C = HG_CHUNK
    r_i = lax.broadcasted_iota(jnp.int32, (C, C), 0)
    c_i = lax.broadcasted_iota(jnp.int32, (C, C), 1)
    tri = {True: (c_i <= r_i).astype(jnp.bfloat16), False: (c_i >= r_i).astype(jnp.bfloat16)}
    r2 = lax.broadcasted_iota(jnp.int32, (LANES, LANES), 0) // HG_DIM
    c2 = lax.broadcasted_iota(jnp.int32, (LANES, LANES), 1) // HG_DIM
    same_head = r2 == c2
    half_ones = same_head.astype(jnp.bfloat16)
    row_id = lax.broadcasted_iota(jnp.int32, (C, LANES), 0)
    lane_lo = lax.broadcasted_iota(jnp.int32, (1, LANES), 1) < HG_DIM

    def chunk(c_idx, s_t, fwd, bufs):
        b_s, k_s, q_s, v_s, oi_s = bufs
        rows = pl.ds(pl.multiple_of(c_idx * C, C), C)
        lb = lb_ref[0:1, :] if fwd else lb_ref[1:2, :]
        z = (zf_ref if fwd else zb_ref)[rows, :]
        f = jnp.maximum(lb + (1.0 - lb) * jax.nn.sigmoid(z), MIN_FORGET)
        lf = jnp.log(f)
        hi, mid, lo = _split3_bf16(lf)
        b = _dot(tri[fwd], hi) + _dot(tri[fwd], mid) + _dot(tri[fwd], lo)
        q = q_ref[rows, :] * (HG_DIM ** -0.5)
        v = i_ref[rows, :]
        b_s[...] = b
        k_s[...] = 1.0 - f
        q_s[...] = q
        v_s[...] = v
        b_last = b[C - 1:C, :] if fwd else b[0:1, :]
        k = 1.0 - f
        G = SUBLANES
        n_groups = C // G
        ref = (lambda g: G * g + G - 1) if fwd else (lambda g: G * g)
        later = (lambda g: slice(G * (g + 1), C)) if fwd else (lambda g: slice(0, G * g))

        tiles = []
        for i in range(C):
            grp = slice(i // G * G, i // G * G + G)
            e = jnp.exp(jnp.minimum(b_s[i:i + 1, :] - b_s[grp, :], 0.0))
            visible = (row_id[0:G, :] <= i % G) if fwd else (row_id[0:G, :] >= i % G)
            tiles.append(jnp.where(visible, e * k_s[grp, :] * q_s[i:i + 1, :], 0.0))
        a_same = _dot(jnp.concatenate(tiles, axis=0).astype(jnp.bfloat16), half_ones)
        for i in range(C):
            grp = slice(i // G * G, i // G * G + G)
            oi_s[i:i + 1, :] = jnp.sum(a_same[G * i:G * i + G, :] * v_s[grp, :], axis=0, keepdims=True)

        b_ref_rows = jnp.concatenate([jnp.broadcast_to(b[ref(g):ref(g) + 1, :], (G, LANES))
                                      for g in range(n_groups)], axis=0)
        kp_t = (k * jnp.exp(b_ref_rows - b)).T
        rhs = jnp.where(same_head, jnp.concatenate([kp_t, kp_t], axis=1), 0.0)
        key_groups = [g for g in range(n_groups) if later(g).start < later(g).stop]
        lhs = jnp.concatenate([q[later(g), :] * jnp.exp(b[later(g), :] - b[ref(g):ref(g) + 1, :])
                               for g in key_groups], axis=0)
        res = _dot(lhs.astype(jnp.bfloat16), rhs.astype(jnp.bfloat16))
        lane_group = (lax.broadcasted_iota(jnp.int32, (G, LANES), 1) % HG_DIM) // G
        offs = np.cumsum([0] + [later(g).stop - later(g).start for g in key_groups])
        a_rows = []
        for blk in range(n_groups):
            acc = jnp.zeros((G, LANES), jnp.float32)
            for n, g in enumerate(key_groups):
                if later(g).start <= G * blk < later(g).stop:
                    r0 = int(offs[n]) + G * blk - later(g).start
                    acc = jnp.where(lane_group == g, res[r0:r0 + G, :], acc)
            a_rows.append(acc)
        a_cross = jnp.concatenate(a_rows, axis=0).astype(jnp.bfloat16)
        v_bd = jnp.where(same_head, jnp.concatenate([v, v], axis=0), 0.0).astype(jnp.bfloat16)

        qe = (q * jnp.exp(b)).astype(jnp.bfloat16)
        o = oi_s[...] + _dot(a_cross, v_bd) + _dot_nt(qe, s_t.astype(jnp.bfloat16))
        ke = (k_s[...] * jnp.exp(b_last - b)).astype(jnp.bfloat16)
        upd = _dot(v.T.astype(jnp.bfloat16), ke)
        s_t = s_t * jnp.exp(b_last) + jnp.where(same_head, upd, 0.0)
        return rows, o, s_t

    def scan_body(n, states):
        rows, o, s_fwd = chunk(n, states[0], True, scratch[0])
        of_s[rows, :] = o
        c_idx = jnp.where(n < HG_CTX_CHUNKS, HG_CTX_CHUNKS - 1 - n, HG_NCHUNK + HG_CTX_CHUNKS - 1 - n)
        rows, o, s_bwd = chunk(c_idx, states[1], False, scratch[1])
        ob_s[rows, :] = o
        return s_fwd, s_bwd

    def readout_body(n, carry):
        rows = pl.ds(pl.multiple_of(n * C, C), C)
        o = of_s[rows, :] + ob_s[rows, :]
        sq = o * o
        ms = jnp.where(lane_lo,
                       jnp.sum(jnp.where(lane_lo, sq, 0.0), axis=-1, keepdims=True),
                       jnp.sum(jnp.where(lane_lo, 0.0, sq), axis=-1, keepdims=True)) * (1.0 / HG_DIM)
        y = o * lax.rsqrt(ms + EPS) * gn_ref[...]
        g = g_ref[rows, :]
        o_ref[rows, :] = (y * (g * jax.nn.sigmoid(g))).astype(o_ref.dtype)
        return carry

    zero = jnp.zeros((LANES, LANES), jnp.float32)
    lax.fori_loop(0, HG_NCHUNK, scan_body, (zero, zero), unroll=2)
    lax.fori_loop(0, HG_NCHUNK, readout_body, 0)


def _hgrn_mixer(p32, lb, onorm_g):
    B = p32.shape[0]
    n_pairs = HG_WIDTH // LANES

    def col(group):
        return pl.BlockSpec((None, TOK, LANES), lambda b, hp, group=group: (b, 0, group * n_pairs + hp))

    gn = jnp.tile(onorm_g.astype(jnp.float32), LANES // HG_DIM).reshape(1, LANES)
    chunk_scratch = pltpu.VMEM((HG_CHUNK, LANES), jnp.float32)
    return pl.pallas_call(
        _hgrn_body,
        grid=(B, n_pairs),
        in_specs=[col(0), col(1), col(2), col(3), col(4),
                  pl.BlockSpec((2, LANES), lambda b, hp: (0, hp)),
                  pl.BlockSpec((1, LANES), lambda b, hp: (0, 0))],
        out_specs=pl.BlockSpec((None, TOK, LANES), lambda b, hp: (b, 0, hp)),
        out_shape=jax.ShapeDtypeStruct((B, TOK, HG_WIDTH), jnp.bfloat16),
        scratch_shapes=[pltpu.VMEM((TOK, LANES), jnp.float32)] * 2 + [chunk_scratch] * (2 * HG_CHUNK_SCRATCH),
        compiler_params=_cparams(2),
        name="hgrn2",
    )(p32, p32, p32, p32, p32, lb.astype(jnp.float32), gn)


MLA_QK_WIDTH = MLA_HEADS * MLA_HEAD_PAD
MLA_QSCALE = (MLA_NOPE + MLA_ROPE) ** -0.5 * math.log2(math.e)
MLA_TQ = 256
MLA_KEY_SLAB = 128


def _mla_prep_body(cq_ref, ckv_ref, kr_ref, krp_ref, cos_ref, sin_ref, gq_ref, gkv_ref,
                   wq_ref, wqr_ref, wk_ref, wv_ref, q_out, k_out, v_out):
    def rms(x, g):
        return (x * lax.rsqrt(jnp.mean(x * x, axis=-1, keepdims=True) + EPS) * g).astype(jnp.bfloat16)

    cos = cos_ref[...]
    sin = sin_ref[...]
    cqn = rms(cq_ref[...], gq_ref[...])
    q = _dot(cqn, wq_ref[...])
    qr = _dot(cqn, wqr_ref[...])
    ckvn = rms(ckv_ref[...], gkv_ref[...])
    kn = _dot(ckvn, wk_ref[...])
    v_out[...] = _dot(ckvn, wv_ref[...]).T.astype(v_out.dtype)
    k_rope = kr_ref[...] * cos + krp_ref[...] * sin
    for h in range(MLA_HEADS):
        sl = slice(h * MLA_HEAD_PAD, (h + 1) * MLA_HEAD_PAD)
        q_out[:, sl] = ((q[:, sl] * cos + qr[:, sl] * sin) * MLA_QSCALE).astype(q_out.dtype)
        k_out[:, sl] = (kn[:, sl] + k_rope).astype(k_out.dtype)


def _mla_weights(w_uq, w_ukv):
    partner, sign = _rope_partner()
    L = w_uq.shape[0]
    wq = w_uq.reshape(L, MLA_Q_RANK, MLA_HEADS, MLA_NOPE + MLA_ROPE)
    q_nope, q_rope = wq[..., :MLA_NOPE], wq[..., MLA_NOPE:]
    pad = jnp.zeros(wq.shape[:3] + (MLA_HEAD_PAD - MLA_NOPE - MLA_ROPE,), w_uq.dtype)
    wq_pad = jnp.concatenate([q_nope, q_rope, pad], axis=-1)
    wq_rot = jnp.concatenate([jnp.zeros_like(q_nope), q_rope[..., partner] * sign, pad], axis=-1)
    wkv = w_ukv.reshape(L, MLA_KV_RANK, MLA_HEADS, MLA_NOPE + MLA_V)
    k_nope, v = wkv[..., :MLA_NOPE], wkv[..., MLA_NOPE:]
    wk_pad = jnp.concatenate([k_nope, jnp.zeros(k_nope.shape[:3] + (MLA_HEAD_PAD - MLA_NOPE,), w_ukv.dtype)], axis=-1)
    bf = lambda a, n: a.reshape(L, a.shape[1], n).astype(jnp.bfloat16)
    return bf(wq_pad, MLA_QK_WIDTH), bf(wq_rot, MLA_QK_WIDTH), bf(wk_pad, MLA_QK_WIDTH), bf(v, MLA_WIDTH)


def _rope_tables():
    t = jnp.arange(SEQ)
    half = MLA_ROPE // 2
    inv = ROPE_BASE ** (-jnp.arange(0, half, 2, dtype=jnp.float32) / half)
    ang_r = (t // GRID_W).astype(jnp.float32)[:, None] * inv
    ang_c = (t % GRID_W).astype(jnp.float32)[:, None] * inv
    ang = jnp.concatenate([ang_r, ang_r, ang_c, ang_c], axis=-1)
    ones = jnp.ones((SEQ, MLA_NOPE), jnp.float32)
    zeros_n = jnp.zeros((SEQ, MLA_NOPE), jnp.float32)
    zeros_p = jnp.zeros((SEQ, MLA_HEAD_PAD - MLA_NOPE - MLA_ROPE), jnp.float32)
    cos_l = jnp.concatenate([ones, jnp.cos(ang), zeros_p], axis=-1)
    sin_l = jnp.concatenate([zeros_n, jnp.sin(ang), zeros_p], axis=-1)
    lane = np.arange(MLA_HEAD_PAD)
    cos_c = jnp.broadcast_to(jnp.asarray((lane < MLA_NOPE + MLA_ROPE).astype(np.float32)), (CTX_LEN, MLA_HEAD_PAD))
    sin_c = jnp.zeros((CTX_LEN, MLA_HEAD_PAD), jnp.float32)
    return jnp.concatenate([cos_c, cos_l], axis=0), jnp.concatenate([sin_c, sin_l], axis=0)


def _mla_prep(p32, cos, sin, gq, gkv, wq, wqr, wk, wv):
    B = p32.shape[0]
    full = lambda a: pl.BlockSpec(a.shape, lambda b, t: (0,) * a.ndim)
    rows = lambda width, blk: pl.BlockSpec((None, ROW_TILE, width), lambda b, t, blk=blk: (b, t, blk))
    tab = pl.BlockSpec((ROW_TILE, MLA_HEAD_PAD), lambda b, t: (t, 0))
    return pl.pallas_call(
        _mla_prep_body,
        grid=(B, N_ROW_TILES),
        in_specs=[rows(MLA_Q_RANK, P32_CQ // MLA_Q_RANK), rows(MLA_KV_RANK, P32_CKV // MLA_KV_RANK),
                  rows(LANES, P32_KR // LANES), rows(LANES, P32_KRP // LANES), tab, tab,
                  full(gq), full(gkv), full(wq), full(wqr), full(wk), full(wv)],
        out_specs=[rows(MLA_QK_WIDTH, 0), rows(MLA_QK_WIDTH, 0),
                   pl.BlockSpec((None, MLA_WIDTH, ROW_TILE), lambda b, t: (b, 0, t))],
        out_shape=[jax.ShapeDtypeStruct((B, TOK, MLA_QK_WIDTH), jnp.bfloat16),
                   jax.ShapeDtypeStruct((B, TOK, MLA_QK_WIDTH), jnp.bfloat16),
                   jax.ShapeDtypeStruct((B, MLA_WIDTH, TOK), jnp.bfloat16)],
        compiler_params=_cparams(2),
        name="mla_prep",
    )(p32, p32, p32, p32, cos, sin, gq, gkv, wq, wqr, wk, wv)


def _mla_attn_body(q_ref, k_ref, vt_ref, o_ref, sa_s, sb_s, ma_s, mb_s):
    heads = [slice(hh * MLA_HEAD_PAD, (hh + 1) * MLA_HEAD_PAD) for hh in range(2)]

    def over_keys(x, op):
        slabs = op(x.reshape(x.shape[0] // MLA_KEY_SLAB, MLA_KEY_SLAB, x.shape[1]), axis=0)
        return op(slabs, axis=0, keepdims=True)

    def qrows_of(j):
        return pl.ds(pl.multiple_of(CTX_LEN + j * MLA_TQ, MLA_TQ), MLA_TQ)

    def scores(j, s_s, m_s):
        for hh in range(2):
            st = _dot_nt(k_ref[:, heads[hh]], q_ref[qrows_of(j), heads[hh]])
            s_s[hh] = st
            m_s[hh, 0:1, :] = over_keys(st, jnp.max)

    def finish(st_of, m_of, qrows, n_keys):
        outs = []
        for hh in range(2):
            p = jnp.exp2(st_of(hh) - m_of(hh))
            den = over_keys(p, jnp.sum)
            ot = _dot(vt_ref[hh * MLA_V:(hh + 1) * MLA_V, 0:n_keys], p.astype(jnp.bfloat16))
            outs.append(ot / den)
        o_ref[qrows, :] = jnp.concatenate(outs, axis=0).T.astype(o_ref.dtype)

    def finish_block(j, s_s, m_s):
        finish(lambda hh: s_s[hh], lambda hh: m_s[hh, 0:1, :], qrows_of(j), TOK)

    n_blocks = SEQ // MLA_TQ
    scores(0, sa_s, ma_s)

    def pair_body(t, carry):
        finish_block(2 * t, sa_s, ma_s)
        scores(2 * t + 1, sb_s, mb_s)
        finish_block(2 * t + 1, sb_s, mb_s)
        scores(jnp.minimum(2 * t + 2, n_blocks - 1), sa_s, ma_s)
        return carry

    lax.fori_loop(0, n_blocks // 2, pair_body, 0)

    ctx_st = [_dot_nt(k_ref[0:CTX_LEN, heads[hh]], q_ref[0:CTX_LEN, heads[hh]]) for hh in range(2)]
    finish(lambda hh: ctx_st[hh], lambda hh: over_keys(ctx_st[hh], jnp.max), slice(0, CTX_LEN), CTX_LEN)


def _mla_attention(q, k, v):
    B = q.shape[0]
    qk = pl.BlockSpec((None, TOK, 2 * MLA_HEAD_PAD), lambda b, hp: (b, 0, hp))
    return pl.pallas_call(
        _mla_attn_body,
        grid=(B, MLA_WIDTH // LANES),
        in_specs=[qk, qk, pl.BlockSpec((None, LANES, TOK), lambda b, hp: (b, hp, 0))],
        out_specs=pl.BlockSpec((None, TOK, LANES), lambda b, hp: (b, 0, hp)),
        out_shape=jax.ShapeDtypeStruct((B, TOK, MLA_WIDTH), jnp.bfloat16),
        scratch_shapes=[pltpu.VMEM((2, TOK, MLA_TQ), jnp.float32)] * 2
        + [pltpu.VMEM((2, SUBLANES, MLA_TQ), jnp.float32)] * 2,
        compiler_params=_cparams(2),
        name="mla_attention",
    )(q, k, v)


def _out_proj_body(hg_ref, ml_ref, na_ref, x_ref, mod_ref, w_ref, gf_ref, wr_ref, o_ref, h_ref, aff_ref):
    o = (_dot(hg_ref[...], w_ref[0:HG_WIDTH, :])
         + _dot(ml_ref[...], w_ref[HG_WIDTH:HG_WIDTH + MLA_WIDTH, :])
         + _dot(na_ref[...], w_ref[HG_WIDTH + MLA_WIDTH:, :]))
    x = x_ref[...] + _mod_row(mod_ref, MOD_GATE_A) * o
    o_ref[...] = x
    h = _modulated_norm(x, gf_ref[...], _mod_row(mod_ref, MOD_SHIFT_F), _mod_row(mod_ref, MOD_SCALE_F))
    h_ref[...] = h
    logits = _dot(h.astype(jnp.bfloat16), wr_ref[...])
    lane = lax.broadcasted_iota(jnp.int32, logits.shape, 1)
    logits = jnp.where(lane < N_EXPERTS, logits, NA_MASKED)
    e = jnp.exp(logits - jnp.max(logits, axis=-1, keepdims=True))
    aff = e / jnp.sum(e, axis=-1, keepdims=True)
    aff_ref[...] = aff.T[0:N_EXPERTS, :]


def _out_proj(hg, ml, na, xc, mods, w, g_ffn, w_router):
    B = xc.shape[0]
    return pl.pallas_call(
        _out_proj_body,
        grid=(B, N_ROW_TILES),
        in_specs=[_row_spec(HG_WIDTH), _row_spec(MLA_WIDTH), _row_spec(NA_WIDTH), _row_spec(D_MODEL),
                  _mod_spec(),
                  pl.BlockSpec((D_MIX, D_MODEL), lambda b, t: (0, 0)),
                  pl.BlockSpec((1, D_MODEL), lambda b, t: (0, 0)),
                  pl.BlockSpec((D_MODEL, LANES), lambda b, t: (0, 0))],
        out_specs=[_row_spec(D_MODEL), _row_spec(D_MODEL),
                   pl.BlockSpec((None, N_EXPERTS, ROW_TILE), lambda b, t: (b, 0, t))],
        out_shape=[jax.ShapeDtypeStruct((B, TOK, D_MODEL), jnp.float32),
                   jax.ShapeDtypeStruct((B, TOK, D_MODEL), jnp.float32),
                   jax.ShapeDtypeStruct((B, N_EXPERTS, TOK), jnp.float32)],
        compiler_params=_cparams(2),
        name="out_proj",
    )(hg, ml, na, xc, mods, w, g_ffn, w_router)


CAP_CTX = EC_CAPACITY * CTX_LEN // N_EXPERTS
CAP_LAT = EC_CAPACITY * SEQ // N_EXPERTS
CAP = CAP_CTX + CAP_LAT
IDX_PAD = -(-CAP // LANES) * LANES
ONE_BITS = 0x3F800000
ROUTE_ROWS = -(-(TOK // LANES * N_EXPERTS) // LANES) * LANES
ROUTE_SPLIT = 32
assert CAP < 2 * ROUTE_SPLIT ** 2


def _prefix_count_blocks(m, upper):
    outs, off = [], jnp.zeros((m.shape[0], 1), jnp.float32)
    for j in range(m.shape[1] // LANES):
        loc = _dot(m[:, j * LANES:(j + 1) * LANES].astype(jnp.bfloat16), upper)
        outs.append(loc + off)
        off = off + loc[:, LANES - 1:LANES]
    return outs


def _prefix_count(m, upper):
    return jnp.concatenate(_prefix_count_blocks(m, upper), axis=1)


def _select_top(a, k, upper):
    rows = a.shape[0]

    def step(_, lohi):
        lo, hi = lohi
        mid = lo + ((hi - lo + 1) >> 1)
        cnt = jnp.sum((a >= pltpu.bitcast(mid, jnp.float32)).astype(jnp.int32), axis=-1, keepdims=True)
        ok = cnt >= k
        return jnp.where(ok, mid, lo), jnp.where(ok, hi, mid - 1)

    thr, _ = lax.fori_loop(0, 31, step, (jnp.zeros((rows, 1), jnp.int32), jnp.full((rows, 1), ONE_BITS, jnp.int32)))
    thr = pltpu.bitcast(thr, jnp.float32)
    gt = (a > thr).astype(jnp.float32)
    eq = (a == thr).astype(jnp.float32)
    need = k - jnp.sum(gt, axis=-1, keepdims=True)
    return gt + eq * (_prefix_count(eq, upper) <= need).astype(jnp.float32)


def _route_body(aff_ref, idx_ref):
    a = aff_ref[...]
    r = lax.broadcasted_iota(jnp.int32, (LANES, LANES), 0)
    c = lax.broadcasted_iota(jnp.int32, (LANES, LANES), 1)
    upper = (r <= c).astype(jnp.bfloat16)
    sel = jnp.concatenate([_select_top(a[:, :CTX_LEN], CAP_CTX, upper),
                           _select_top(a[:, CTX_LEN:], CAP_LAT, upper)], axis=1)
    blocks = _prefix_count_blocks(sel, upper)
    pos_rows = jnp.concatenate(blocks + [jnp.zeros((ROUTE_ROWS - len(blocks) * N_EXPERTS, LANES), jnp.float32)],
                               axis=0)
    pos_hi = jnp.floor(pos_rows * (1.0 / ROUTE_SPLIT))
    pos_lo = pos_rows - pos_hi * ROUTE_SPLIT
    pos_digits = jnp.concatenate([pos_hi, pos_lo], axis=1).astype(jnp.bfloat16)
    tok_block = lax.broadcasted_iota(jnp.int32, (TOK, LANES), 0) // LANES
    block_cum = _dot(sel.astype(jnp.bfloat16),
                     (tok_block <= lax.broadcasted_iota(jnp.int32, (TOK, LANES), 1)).astype(jnp.bfloat16))
    lane = lax.broadcasted_iota(jnp.int32, (LANES, LANES), 1)
    row_id = lax.broadcasted_iota(jnp.int32, (LANES, ROUTE_ROWS), 1)
    for sb in range(IDX_PAD // LANES):
        slot = (lax.broadcasted_iota(jnp.int32, (LANES, 1), 0) + sb * LANES).astype(jnp.float32)
        n_before = [jnp.sum((block_cum[e:e + 1, :] <= slot).astype(jnp.float32), axis=-1, keepdims=True)
                    for e in range(N_EXPERTS)]
        pick = jnp.concatenate([(row_id == n_before[e].astype(jnp.int32) * N_EXPERTS + e).astype(jnp.bfloat16)
                                for e in range(N_EXPERTS)], axis=0)
        digits = _dot(pick, pos_digits)
        pos_blk = digits[:, :LANES] * ROUTE_SPLIT + digits[:, LANES:]
        cols = jnp.zeros((LANES, LANES), jnp.float32)
        for e in range(N_EXPERTS):
            inside = jnp.sum((pos_blk[e * LANES:(e + 1) * LANES, :] <= slot).astype(jnp.float32),
                             axis=-1, keepdims=True)
            cols = jnp.where(lane == e, n_before[e] * LANES + inside, cols)
        idx = jnp.minimum(cols.T[0:N_EXPERTS, :], TOK - 1)
        idx_ref[:, sb * LANES:(sb + 1) * LANES] = idx.astype(jnp.int32)


def _route(aff):
    B = aff.shape[0]
    return pl.pallas_call(
        _route_body,
        grid=(B,),
        in_specs=[pl.BlockSpec((None, N_EXPERTS, TOK), lambda b: (b, 0, 0))],
        out_specs=pl.BlockSpec((None, N_EXPERTS, IDX_PAD), lambda b: (b, 0, 0)),
        out_shape=jax.ShapeDtypeStruct((B, N_EXPERTS, IDX_PAD), jnp.int32),
        compiler_params=_cparams(1),
        name="route",
    )(aff)


SCATTER_BATCH = 4


def _smem_row(n):
    return pl.BlockSpec((None, None, 1, n), lambda b, e: (b, e, 0, 0), memory_space=pltpu.SMEM)


def _slot_spec():
    return pl.BlockSpec((None, None, CAP, D_MODEL), lambda b, e: (b, e, 0, 0))


def _gather_body(idx_ref, h_ref, xs_ref, xs_s):
    for s in range(CAP):
        xs_s[s:s + 1, :] = h_ref[pl.ds(idx_ref[0, s], 1), :]
    xs_ref[...] = xs_s[...].astype(xs_ref.dtype)


def _swiglu_body(xs_ref, w1_ref, w3_ref, w2_ref, y_ref):
    xs = xs_ref[...]
    a = _dot(xs, w1_ref[...])
    u = _dot(xs, w3_ref[...])
    y_ref[...] = _dot((a * jax.nn.sigmoid(a) * u).astype(jnp.bfloat16), w2_ref[...])


def _scatter_body(idx_ref, aff_ref, y_ref, acc_ref):
    @pl.when(pl.program_id(1) == 0)
    def _():
        acc_ref[...] = jnp.zeros_like(acc_ref)

    for s0 in range(0, CAP, SCATTER_BATCH):
        rows = [idx_ref[0, s0 + j] for j in range(SCATTER_BATCH)]
        new = [acc_ref[pl.ds(rows[j], 1), :] + aff_ref[0, rows[j]] * y_ref[s0 + j:s0 + j + 1, :]
               for j in range(SCATTER_BATCH)]
        for j in range(SCATTER_BATCH):
            acc_ref[pl.ds(rows[j], 1), :] = new[j]


def _experts(idx, aff, h, w1, w3, w2):
    B = h.shape[0]
    grid = (B, N_EXPERTS)
    idx4, aff4 = idx[:, :, None, :], aff[:, :, None, :]
    sample = pl.BlockSpec((None, TOK, D_MODEL), lambda b, e: (b, 0, 0))
    wspec = lambda k, n: pl.BlockSpec((None, k, n), lambda b, e: (e, 0, 0))
    slots = lambda dtype: jax.ShapeDtypeStruct((B, N_EXPERTS, CAP, D_MODEL), dtype)
    xs = pl.pallas_call(
        _gather_body, grid=grid,
        in_specs=[_smem_row(IDX_PAD), sample],
        out_specs=_slot_spec(), out_shape=slots(jnp.bfloat16),
        scratch_shapes=[pltpu.VMEM((CAP, D_MODEL), jnp.float32)],
        compiler_params=_cparams(2), name="expert_gather",
    )(idx4, h)
    y = pl.pallas_call(
        _swiglu_body, grid=grid,
        in_specs=[_slot_spec(), wspec(D_MODEL, F_EXPERT), wspec(D_MODEL, F_EXPERT), wspec(F_EXPERT, D_MODEL)],
        out_specs=_slot_spec(), out_shape=slots(jnp.float32),
        compiler_params=_cparams(2), name="expert_swiglu",
    )(xs, w1, w3, w2)
    return pl.pallas_call(
        _scatter_body, grid=grid,
        in_specs=[_smem_row(IDX_PAD), _smem_row(TOK), _slot_spec()],
        out_specs=sample, out_shape=jax.ShapeDtypeStruct((B, TOK, D_MODEL), jnp.float32),
        compiler_params=_cparams(2), name="expert_scatter",
    )(idx4, aff4, y)


def _final_body(x_ref, delta_ref, mod_ref, g_ref, o_ref):
    x = x_ref[...] + _mod_row(mod_ref, MOD_GATE_F) * delta_ref[...]
    o_ref[...] = x * lax.rsqrt(jnp.mean(x * x, axis=-1, keepdims=True) + EPS) * g_ref[...]


def _final_norm(xc, delta, mods, g):
    B = xc.shape[0]
    lat_rows = pl.BlockSpec((None, ROW_TILE, D_MODEL), lambda b, t: (b, t + CTX_TILES, 0))
    return pl.pallas_call(
        _final_body,
        grid=(B, SEQ // ROW_TILE),
        in_specs=[lat_rows, lat_rows,
                  pl.BlockSpec((None, None, 6, D_MODEL), lambda b, t: (b, 1, 0, 0)),
                  pl.BlockSpec((1, D_MODEL), lambda b, t: (0, 0))],
        out_specs=pl.BlockSpec((None, ROW_TILE, D_MODEL), lambda b, t: (b, t, 0)),
        out_shape=jax.ShapeDtypeStruct((B, SEQ, D_MODEL), jnp.float32),
        compiler_params=_cparams(2),
        name="final_norm",
    )(xc, delta, mods, g)


def kernel(x, c, ctx, c_ctx, ada_w, ada_b, norm_mix_g, norm_ffn_g, w_in, hgrn_lb_logits,
           hgrn_onorm_g, mla_qnorm_g, mla_w_uq, mla_kvnorm_g, mla_w_ukv, na_rpb, w_out,
           router_w, exp_w1, exp_w3, exp_w2, final_norm_g):
    B = x.shape[0]
    lb_w = jax.nn.softmax(hgrn_lb_logits.astype(jnp.float32), axis=1)
    lb_all = jnp.cumsum(lb_w, axis=1) - lb_w[:, :1]
    s_lat = jax.nn.silu(c)
    s_ctx = jax.nn.silu(c_ctx)
    w_in_b = _widen_w_in(w_in)
    w_out_b = w_out.astype(jnp.bfloat16)
    wq, wqr, wk, wv = _mla_weights(mla_w_uq, mla_w_ukv)
    cos, sin = _rope_tables()
    w_router = jnp.pad(router_w, ((0, 0), (0, 0), (0, LANES - N_EXPERTS))).astype(jnp.bfloat16)
    w1_b, w3_b, w2_b = (w.astype(jnp.bfloat16) for w in (exp_w1, exp_w3, exp_w2))
    row = lambda g: g.reshape(1, -1).astype(jnp.float32)
    xc = jnp.concatenate([ctx, x], axis=1)
    delta, prev_mods = None, None
    for layer in range(DEPTH):
        mod_l = (s_lat @ ada_w[layer] + ada_b[layer]).reshape(B, 6, D_MODEL)
        mod_c = jnp.broadcast_to((s_ctx @ ada_w[layer] + ada_b[layer]).reshape(1, 6, D_MODEL), (B, 6, D_MODEL))
        mods = jnp.stack([mod_c, mod_l], axis=1)

        xc, p32, p16 = _in_proj(xc, delta, prev_mods, row(norm_mix_g[layer]), mods, w_in_b[layer])
        hg = _hgrn_mixer(p32, lb_all[:, layer], hgrn_onorm_g[layer])
        q, k, v = _mla_prep(p32, cos, sin, row(mla_qnorm_g[layer]), row(mla_kvnorm_g[layer]),
                            wq[layer], wqr[layer], wk[layer], wv[layer])
        ml = _mla_attention(q, k, v)
        na = _na_attention(p16, _na_bias_tables(na_rpb[layer]))
        xc, h, aff = _out_proj(hg, ml, na, xc, mods, w_out_b[layer], row(norm_ffn_g[layer]), w_router[layer])
        delta = _experts(_route(aff), aff, h, w1_b[layer], w3_b[layer], w2_b[layer])
        prev_mods = mods
    return _final_norm(xc, delta, prev_mods, row(final_norm_g))
```

```python
import functools
import math

import numpy as np
import jax
import jax.numpy as jnp
from jax import lax
from jax.experimental import pallas as pl
from jax.experimental.pallas import tpu as pltpu

D_MODEL = 1024
BATCH = 8
SEQ = 4096
DEPTH = 4

CTX_LEN = 256
GRID_W = 64
EPS = 1e-6
MIN_FORGET = 1e-20

D_MIX = D_MODEL
HG_HEADS = 4
HG_DIM = 64
HG_WIDTH = HG_HEADS * HG_DIM
HG_CHUNK = 64
MLA_HEADS = 6
MLA_NOPE = 64
MLA_ROPE = 32
MLA_V = 64
MLA_Q_RANK = 256
MLA_KV_RANK = 128
MLA_WIDTH = MLA_HEADS * MLA_V
NA_HEADS = 6
NA_DIM = 64
NA_WIDTH = NA_HEADS * NA_DIM
NA_WIN_R = 8
NA_WIN_C = 16
HG_IN = 5 * HG_WIDTH
MLA_IN = MLA_Q_RANK + MLA_KV_RANK + MLA_ROPE
NA_IN = 3 * NA_WIDTH
D_IN = HG_IN + MLA_IN + NA_IN
ROPE_BASE = 10000.0
N_EXPERTS = 16
EC_CAPACITY = 2
F_EXPERT = 1024

LANES = 128
SUBLANES = 8
VMEM_LIMIT_BYTES = 56 * 1024 * 1024

TOK = CTX_LEN + SEQ
ROW_TILE = 256
N_ROW_TILES = TOK // ROW_TILE
CTX_TILES = CTX_LEN // ROW_TILE

MLA_HEAD_PAD = LANES
P32_HG = 0
P32_CQ = HG_IN
P32_CKV = P32_CQ + MLA_Q_RANK
P32_KR = P32_CKV + MLA_KV_RANK
P32_KRP = P32_KR + LANES
P32_WIDTH = P32_KRP + LANES
P_WIDTH = P32_WIDTH + NA_IN


def _cparams(n_grid_dims):
    return pltpu.CompilerParams(dimension_semantics=("arbitrary",) * n_grid_dims,
                                vmem_limit_bytes=VMEM_LIMIT_BYTES)


def _dot_nt(a, b):
    return lax.dot_general(a, b, (((1,), (1,)), ((), ())), preferred_element_type=jnp.float32)


def _dot(a, b):
    return jnp.dot(a, b, preferred_element_type=jnp.float32)


def _kind_of_tile(t):
    return (t >= CTX_TILES).astype(jnp.int32)


MOD_SHIFT_A, MOD_SCALE_A, MOD_GATE_A, MOD_SHIFT_F, MOD_SCALE_F, MOD_GATE_F = range(6)


def _mod_row(mod_ref, i):
    return mod_ref[i:i + 1, :]


def _modulated_norm(x, g, shift, scale):
    y = x * lax.rsqrt(jnp.mean(x * x, axis=-1, keepdims=True) + EPS) * g
    return y * (1.0 + scale) + shift


def _in_proj_body(*refs, has_delta):
    if has_delta:
        x_ref, delta_ref, pmod_ref, g_ref, mod_ref, w_ref, xo_ref, p32_ref, p16_ref = refs
        x = x_ref[...] + _mod_row(pmod_ref, MOD_GATE_F) * delta_ref[...]
        xo_ref[...] = x
    else:
        x_ref, g_ref, mod_ref, w_ref, p32_ref, p16_ref = refs
        x = x_ref[...]
    h = _modulated_norm(x, g_ref[...], _mod_row(mod_ref, MOD_SHIFT_A), _mod_row(mod_ref, MOD_SCALE_A))
    p = _dot(h.astype(jnp.bfloat16), w_ref[...])
    p32_ref[...] = p[:, :P32_WIDTH]
    p16_ref[...] = p[:, P32_WIDTH:].astype(p16_ref.dtype)


def _row_spec(width):
    return pl.BlockSpec((None, ROW_TILE, width), lambda b, t: (b, t, 0))


def _mod_spec():
    return pl.BlockSpec((None, None, 6, D_MODEL), lambda b, t: (b, _kind_of_tile(t), 0, 0))


def _in_proj(xc, delta, prev_mods, g, mods, w):
    B = xc.shape[0]
    has_delta = delta is not None
    fixed = [pl.BlockSpec((1, D_MODEL), lambda b, t: (0, 0)), _mod_spec(),
             pl.BlockSpec((D_MODEL, P_WIDTH), lambda b, t: (0, 0))]
    outs = [_row_spec(P32_WIDTH), _row_spec(NA_IN)]
    shapes = [jax.ShapeDtypeStruct((B, TOK, P32_WIDTH), jnp.float32),
              jax.ShapeDtypeStruct((B, TOK, NA_IN), jnp.bfloat16)]
    if has_delta:
        in_specs = [_row_spec(D_MODEL), _row_spec(D_MODEL), _mod_spec()] + fixed
        args = (xc, delta, prev_mods, g, mods, w)
        outs = [_row_spec(D_MODEL)] + outs
        shapes = [jax.ShapeDtypeStruct((B, TOK, D_MODEL), jnp.float32)] + shapes
    else:
        in_specs = [_row_spec(D_MODEL)] + fixed
        args = (xc, g, mods, w)
    res = pl.pallas_call(
        functools.partial(_in_proj_body, has_delta=has_delta),
        grid=(B, N_ROW_TILES),
        in_specs=in_specs,
        out_specs=outs,
        out_shape=shapes,
        compiler_params=_cparams(2),
        name="in_proj",
    )(*args)
    return res if has_delta else [xc] + list(res)


def _rope_partner():
    half = MLA_ROPE // 2
    quarter = half // 2
    partner, sign = [], []
    for r in range(MLA_ROPE):
        base, j = (r // half) * half, r % half
        partner.append(base + (j + quarter if j < quarter else j - quarter))
        sign.append(-1.0 if j < quarter else 1.0)
    return np.array(partner), np.array(sign, np.float32)


def _widen_w_in(w_in):
    L = w_in.shape[0]
    a0, a1 = HG_IN, HG_IN + MLA_IN
    partner, sign = _rope_partner()
    kr = w_in[..., a0 + MLA_Q_RANK + MLA_KV_RANK:a1]
    zl = jnp.zeros((L, D_MODEL, MLA_NOPE), w_in.dtype)
    zr = jnp.zeros((L, D_MODEL, MLA_HEAD_PAD - MLA_NOPE - MLA_ROPE), w_in.dtype)
    na = w_in[..., a1:]
    return jnp.concatenate([
        w_in[..., :a0 + MLA_Q_RANK + MLA_KV_RANK],
        zl, kr, zr,
        zl, kr[..., partner] * sign, zr,
        na[..., :NA_WIDTH] * NA_DIM ** -0.5, na[..., NA_WIDTH:]], axis=-1).astype(jnp.bfloat16)


GRID_H = SEQ // GRID_W
NA_QROWS = 4
NA_QB = NA_QROWS * GRID_W
NA_KROWS = NA_QROWS + NA_WIN_R - 1
NA_KB = NA_KROWS * GRID_W
NA_NBLK = GRID_H // NA_QROWS
NA_MASKED = -1e30


def _na_window_start(m):
    return min(max(NA_QROWS * m - NA_WIN_R // 2, 0), GRID_H - NA_KROWS)


NA_NDROW = 2 * NA_WIN_R - 1
NA_NDCOL = 2 * NA_WIN_C - 1


@functools.lru_cache(maxsize=None)
def _na_patterns():
    pats, pat_of_block = [], []
    for m in range(NA_NBLK):
        pat = np.full((NA_QROWS, NA_KROWS), NA_NDROW, np.int64)
        for qr in range(NA_QROWS):
            r = NA_QROWS * m + qr
            rs = min(max(r - NA_WIN_R // 2, 0), GRID_H - NA_WIN_R)
            for j in range(NA_KROWS):
                kr = _na_window_start(m) + j
                if rs <= kr < rs + NA_WIN_R:
                    pat[qr, j] = kr - r + NA_WIN_R - 1
        for p, other in enumerate(pats):
            if (other == pat).all():
                pat_of_block.append(p)
                break
        else:
            pat_of_block.append(len(pats))
            pats.append(pat)
    return np.stack(pats), tuple(pat_of_block)


def _na_bias_tables(rpb):
    pats, _ = _na_patterns()
    c = np.arange(GRID_W)[:, None]
    kc = np.arange(GRID_W)[None, :]
    cs = np.clip(c - NA_WIN_C // 2, 0, GRID_W - NA_WIN_C)
    col_valid = (kc >= cs) & (kc < cs + NA_WIN_C)
    onehot = (np.arange(NA_NDCOL)[:, None, None] == (kc - c + NA_WIN_C - 1)[None]).astype(np.float32)
    tiles = jnp.einsum('hdj,jck->hdck', rpb.astype(jnp.float32), onehot, precision=lax.Precision.HIGHEST)
    tiles = jnp.where(col_valid, tiles, NA_MASKED)
    tiles = jnp.concatenate([tiles, jnp.full((NA_HEADS, 1, GRID_W, GRID_W), NA_MASKED, jnp.float32)], axis=1)
    rows = [jnp.concatenate([tiles[:, int(d)] for d in pat_row], axis=-1) for pat in pats for pat_row in pat]
    return jnp.stack(rows, axis=1).reshape(NA_HEADS, len(pats), NA_QB, NA_KB)


def _na_body(q_ref, k_ref, v_ref, bias_ref, o_ref):
    _, pat_of_block = _na_patterns()
    lane = lax.broadcasted_iota(jnp.int32, (1, LANES), 1)
    head_masks = [(lane >= NA_DIM * hh) & (lane < NA_DIM * (hh + 1)) for hh in range(2)]
    kc = k_ref[0:CTX_LEN, :]
    vc = v_ref[0:CTX_LEN, :]
    vc_h = [jnp.where(hm, vc, jnp.zeros_like(vc)) for hm in head_masks]

    def block(m, ws, pat):
        qrows = pl.ds(pl.multiple_of(CTX_LEN + m * NA_QB, NA_QB), NA_QB)
        q = q_ref[qrows, :]
        krows = pl.ds(pl.multiple_of(CTX_LEN + ws * GRID_W, GRID_W), NA_KB)
        kwin = k_ref[krows, :]
        vwin = v_ref[krows, :]
        out = None
        for hh in range(2):
            qh = jnp.where(head_masks[hh], q, jnp.zeros_like(q))
            s_w = _dot_nt(qh, kwin) + bias_ref[hh, pat]
            s_c = _dot_nt(qh, kc)
            mx = jnp.maximum(jnp.max(s_w, axis=-1, keepdims=True), jnp.max(s_c, axis=-1, keepdims=True))
            p_w = jnp.exp(s_w - mx)
            p_c = jnp.exp(s_c - mx)
            den = jnp.sum(p_w, axis=-1, keepdims=True) + jnp.sum(p_c, axis=-1, keepdims=True)
            vh = jnp.where(head_masks[hh], vwin, jnp.zeros_like(vwin))
            o = _dot(p_w.astype(jnp.bfloat16), vh) + _dot(p_c.astype(jnp.bfloat16), vc_h[hh])
            o = o / den
            out = o if out is None else out + o
        o_ref[qrows, :] = out.astype(o_ref.dtype)

    interior = [m for m in range(NA_NBLK) if pat_of_block.count(pat_of_block[m]) > 1]
    lo, hi = interior[0], interior[-1] + 1
    assert interior == list(range(lo, hi)) and len({pat_of_block[m] for m in interior}) == 1
    for m in list(range(lo)) + list(range(hi, NA_NBLK)):
        block(m, _na_window_start(m), pat_of_block[m])

    def loop_body(m, carry):
        block(m, NA_QROWS * m - NA_WIN_R // 2, pat_of_block[lo])
        return carry

    lax.fori_loop(lo, hi, loop_body, 0, unroll=2)

    qc = q_ref[0:CTX_LEN, :]
    out = None
    for hh in range(2):
        qh = jnp.where(head_masks[hh], qc, jnp.zeros_like(qc))
        s = _dot_nt(qh, kc)
        p = jnp.exp(s - jnp.max(s, axis=-1, keepdims=True))
        o = _dot(p.astype(jnp.bfloat16), vc_h[hh]) / jnp.sum(p, axis=-1, keepdims=True)
        out = o if out is None else out + o
    o_ref[0:CTX_LEN, :] = out.astype(o_ref.dtype)


def _na_attention(p16, bias):
    B = p16.shape[0]
    n_pat = bias.shape[1]
    n_pairs = NA_WIDTH // LANES

    def col(group):
        return pl.BlockSpec((None, TOK, LANES), lambda b, hp, group=group: (b, 0, group * n_pairs + hp))

    return pl.pallas_call(
        _na_body,
        grid=(B, n_pairs),
        in_specs=[col(0), col(1), col(2),
                  pl.BlockSpec((2, n_pat, NA_QB, NA_KB), lambda b, hp: (hp, 0, 0, 0))],
        out_specs=col(0),
        out_shape=jax.ShapeDtypeStruct((B, TOK, NA_WIDTH), jnp.bfloat16),
        compiler_params=_cparams(2),
        name="na_attention",
    )(p16, p16, p16, bias)


HG_NCHUNK = TOK // HG_CHUNK
HG_CTX_CHUNKS = CTX_LEN // HG_CHUNK


def _split3_bf16(x):
    hi = x.astype(jnp.bfloat16)
    r1 = x - hi.astype(jnp.float32)
    mid = r1.astype(jnp.bfloat16)
    lo = (r1 - mid.astype(jnp.float32)).astype(jnp.bfloat16)
    return hi, mid, lo


HG_CHUNK_SCRATCH = 5


def _hgrn_body(q_ref, i_ref, zf_ref, zb_ref, g_ref, lb_ref, gn_ref, o_ref, of_s, ob_s, *chunk_scratch):
    scratch = (chunk_scratch[:HG_CHUNK_SCRATCH], chunk_scratch[HG_CHUNK_SCRATCH:])
    C = HG_CHUNK
    r_i = lax.broadcasted_iota(jnp.int32, (C, C), 0)
    c_i = lax.broadcasted_iota(jnp.int32, (C, C), 1)
    tri = {True: (c_i <= r_i).astype(jnp.bfloat16), False: (c_i >= r_i).astype(jnp.bfloat16)}
    r2 = lax.broadcasted_iota(jnp.int32, (LANES, LANES), 0) // HG_DIM
    c2 = lax.broadcasted_iota(jnp.int32, (LANES, LANES), 1) // HG_DIM
    same_head = r2 == c2
    half_ones = same_head.astype(jnp.bfloat16)
    row_id = lax.broadcasted_iota(jnp.int32, (C, LANES), 0)
    lane_lo = lax.broadcasted_iota(jnp.int32, (1, LANES), 1) < HG_DIM

    def chunk(c_idx, s_t, fwd, bufs):
        b_s, k_s, q_s, v_s, oi_s = bufs
        rows = pl.ds(pl.multiple_of(c_idx * C, C), C)
        lb = lb_ref[0:1, :] if fwd else lb_ref[1:2, :]
        z = (zf_ref if fwd else zb_ref)[rows, :]
        f = jnp.maximum(lb + (1.0 - lb) * jax.nn.sigmoid(z), MIN_FORGET)
        lf = jnp.log(f)
        hi, mid, lo = _split3_bf16(lf)
        b = _dot(tri[fwd], hi) + _dot(tri[fwd], mid) + _dot(tri[fwd], lo)
        q = q_ref[rows, :] * (HG_DIM ** -0.5)
        v = i_ref[rows, :]
        b_s[...] = b
        k_s[...] = 1.0 - f
        q_s[...] = q
        v_s[...] = v
        b_last = b[C - 1:C, :] if fwd else b[0:1, :]
        k = 1.0 - f
        G = SUBLANES
        n_groups = C // G
        ref = (lambda g: G * g + G - 1) if fwd else (lambda g: G * g)
        later = (lambda g: slice(G * (g + 1), C)) if fwd else (lambda g: slice(0, G * g))

        tiles = []
        for i in range(C):
            grp = slice(i // G * G, i // G * G + G)
            e = jnp.exp(jnp.minimum(b_s[i:i + 1, :] - b_s[grp, :], 0.0))
            visible = (row_id[0:G, :] <= i % G) if fwd else (row_id[0:G, :] >= i % G)
            tiles.append(jnp.where(visible, e * k_s[grp, :] * q_s[i:i + 1, :], 0.0))
        a_same = _dot(jnp.concatenate(tiles, axis=0).astype(jnp.bfloat16), half_ones)
        for i in range(C):
            grp = slice(i // G * G, i // G * G + G)
            oi_s[i:i + 1, :] = jnp.sum(a_same[G * i:G * i + G, :] * v_s[grp, :], axis=0, keepdims=True)

        b_ref_rows = jnp.concatenate([jnp.broadcast_to(b[ref(g):ref(g) + 1, :], (G, LANES))
                                      for g in range(n_groups)], axis=0)
        kp_t = (k * jnp.exp(b_ref_rows - b)).T
        rhs = jnp.where(same_head, jnp.concatenate([kp_t, kp_t], axis=1), 0.0)
        key_groups = [g for g in range(n_groups) if later(g).start < later(g).stop]
        lhs = jnp.concatenate([q[later(g), :] * jnp.exp(b[later(g), :] - b[ref(g):ref(g) + 1, :])
                               for g in key_groups], axis=0)
        res = _dot(lhs.astype(jnp.bfloat16), rhs.astype(jnp.bfloat16))
        lane_group = (lax.broadcasted_iota(jnp.int32, (G, LANES), 1) % HG_DIM) // G
        offs = np.cumsum([0] + [later(g).stop - later(g).start for g in key_groups])
        a_rows = []
        for blk in range(n_groups):
            acc = jnp.zeros((G, LANES), jnp.float32)
            for n, g in enumerate(key_groups):
                if later(g).start <= G * blk < later(g).stop:
                    r0 = int(offs[n]) + G * blk - later(g).start
                    acc = jnp.where(lane_group == g, res[r0:r0 + G, :], acc)
            a_rows.append(acc)
        a_cross = jnp.concatenate(a_rows, axis=0).astype(jnp.bfloat16)
        v_bd = jnp.where(same_head, jnp.concatenate([v, v], axis=0), 0.0).astype(jnp.bfloat16)

        qe = (q * jnp.exp(b)).astype(jnp.bfloat16)
        o = oi_s[...] + _dot(a_cross, v_bd) + _dot_nt(qe, s_t.astype(jnp.bfloat16))
        ke = (k_s[...] * jnp.exp(b_last - b)).astype(jnp.bfloat16)
        upd = _dot(v.T.astype(jnp.bfloat16), ke)
        s_t = s_t * jnp.exp(b_last) + jnp.where(same_head, upd, 0.0)
        return rows, o, s_t

    def scan_body(n, states):
        rows, o, s_fwd = chunk(n, states[0], True, scratch[0])
        of_s[rows, :] = o
        c_idx = jnp.where(n < HG_CTX_CHUNKS, HG_CTX_CHUNKS - 1 - n, HG_NCHUNK + HG_CTX_CHUNKS - 1 - n)
        rows, o, s_bwd = chunk(c_idx, states[1], False, scratch[1])
        ob_s[rows, :] = o
        return s_fwd, s_bwd

    def readout_body(n, carry):
        rows = pl.ds(pl.multiple_of(n * C, C), C)
        o = of_s[rows, :] + ob_s[rows, :]
        sq = o * o
        ms = jnp.where(lane_lo,
                       jnp.sum(jnp.where(lane_lo, sq, 0.0), axis=-1, keepdims=True),
                       jnp.sum(jnp.where(lane_lo, 0.0, sq), axis=-1, keepdims=True)) * (1.0 / HG_DIM)
        y = o * lax.rsqrt(ms + EPS) * gn_ref[...]
        g = g_ref[rows, :]
        o_ref[rows, :] = (y * (g * jax.nn.sigmoid(g))).astype(o_ref.dtype)
        return carry

    zero = jnp.zeros((LANES, LANES), jnp.float32)
    lax.fori_loop(0, HG_NCHUNK, scan_body, (zero, zero), unroll=4)
    lax.fori_loop(0, HG_NCHUNK, readout_body, 0)


def _hgrn_mixer(p32, lb, onorm_g):
    B = p32.shape[0]
    n_pairs = HG_WIDTH // LANES

    def col(group):
        return pl.BlockSpec((None, TOK, LANES), lambda b, hp, group=group: (b, 0, group * n_pairs + hp))

    gn = jnp.tile(onorm_g.astype(jnp.float32), LANES // HG_DIM).reshape(1, LANES)
    chunk_scratch = pltpu.VMEM((HG_CHUNK, LANES), jnp.float32)
    return pl.pallas_call(
        _hgrn_body,
        grid=(B, n_pairs),
        in_specs=[col(0), col(1), col(2), col(3), col(4),
                  pl.BlockSpec((2, LANES), lambda b, hp: (0, hp)),
                  pl.BlockSpec((1, LANES), lambda b, hp: (0, 0))],
        out_specs=pl.BlockSpec((None, TOK, LANES), lambda b, hp: (b, 0, hp)),
        out_shape=jax.ShapeDtypeStruct((B, TOK, HG_WIDTH), jnp.bfloat16),
        scratch_shapes=[pltpu.VMEM((TOK, LANES), jnp.float32)] * 2 + [chunk_scratch] * (2 * HG_CHUNK_SCRATCH),
        compiler_params=_cparams(2),
        name="hgrn2",
    )(p32, p32, p32, p32, p32, lb.astype(jnp.float32), gn)


MLA_QK_WIDTH = MLA_HEADS * MLA_HEAD_PAD
MLA_QSCALE = (MLA_NOPE + MLA_ROPE) ** -0.5 * math.log2(math.e)
MLA_TQ = 256
MLA_KEY_SLAB = 128


def _mla_prep_body(cq_ref, ckv_ref, kr_ref, krp_ref, cos_ref, sin_ref, gq_ref, gkv_ref,
                   wq_ref, wqr_ref, wk_ref, wv_ref, q_out, k_out, v_out):
    def rms(x, g):
        return (x * lax.rsqrt(jnp.mean(x * x, axis=-1, keepdims=True) + EPS) * g).astype(jnp.bfloat16)

    cos = cos_ref[...]
    sin = sin_ref[...]
    cqn = rms(cq_ref[...], gq_ref[...])
    q = _dot(cqn, wq_ref[...])
    qr = _dot(cqn, wqr_ref[...])
    ckvn = rms(ckv_ref[...], gkv_ref[...])
    kn = _dot(ckvn, wk_ref[...])
    v_out[...] = _dot(ckvn, wv_ref[...]).T.astype(v_out.dtype)
    k_rope = kr_ref[...] * cos + krp_ref[...] * sin
    for h in range(MLA_HEADS):
        sl = slice(h * MLA_HEAD_PAD, (h + 1) * MLA_HEAD_PAD)
        q_out[:, sl] = ((q[:, sl] * cos + qr[:, sl] * sin) * MLA_QSCALE).astype(q_out.dtype)
        k_out[:, sl] = (kn[:, sl] + k_rope).astype(k_out.dtype)


def _mla_weights(w_uq, w_ukv):
    partner, sign = _rope_partner()
    L = w_uq.shape[0]
    wq = w_uq.reshape(L, MLA_Q_RANK, MLA_HEADS, MLA_NOPE + MLA_ROPE)
    q_nope, q_rope = wq[..., :MLA_NOPE], wq[..., MLA_NOPE:]
    pad = jnp.zeros(wq.shape[:3] + (MLA_HEAD_PAD - MLA_NOPE - MLA_ROPE,), w_uq.dtype)
    wq_pad = jnp.concatenate([q_nope, q_rope, pad], axis=-1)
    wq_rot = jnp.concatenate([jnp.zeros_like(q_nope), q_rope[..., partner] * sign, pad], axis=-1)
    wkv = w_ukv.reshape(L, MLA_KV_RANK, MLA_HEADS, MLA_NOPE + MLA_V)
    k_nope, v = wkv[..., :MLA_NOPE], wkv[..., MLA_NOPE:]
    wk_pad = jnp.concatenate([k_nope, jnp.zeros(k_nope.shape[:3] + (MLA_HEAD_PAD - MLA_NOPE,), w_ukv.dtype)], axis=-1)
    bf = lambda a, n: a.reshape(L, a.shape[1], n).astype(jnp.bfloat16)
    return bf(wq_pad, MLA_QK_WIDTH), bf(wq_rot, MLA_QK_WIDTH), bf(wk_pad, MLA_QK_WIDTH), bf(v, MLA_WIDTH)


def _rope_tables():
    t = jnp.arange(SEQ)
    half = MLA_ROPE // 2
    inv = ROPE_BASE ** (-jnp.arange(0, half, 2, dtype=jnp.float32) / half)
    ang_r = (t // GRID_W).astype(jnp.float32)[:, None] * inv
    ang_c = (t % GRID_W).astype(jnp.float32)[:, None] * inv
    ang = jnp.concatenate([ang_r, ang_r, ang_c, ang_c], axis=-1)
    ones = jnp.ones((SEQ, MLA_NOPE), jnp.float32)
    zeros_n = jnp.zeros((SEQ, MLA_NOPE), jnp.float32)
    zeros_p = jnp.zeros((SEQ, MLA_HEAD_PAD - MLA_NOPE - MLA_ROPE), jnp.float32)
    cos_l = jnp.concatenate([ones, jnp.cos(ang), zeros_p], axis=-1)
    sin_l = jnp.concatenate([zeros_n, jnp.sin(ang), zeros_p], axis=-1)
    lane = np.arange(MLA_HEAD_PAD)
    cos_c = jnp.broadcast_to(jnp.asarray((lane < MLA_NOPE + MLA_ROPE).astype(np.float32)), (CTX_LEN, MLA_HEAD_PAD))
    sin_c = jnp.zeros((CTX_LEN, MLA_HEAD_PAD), jnp.float32)
    return jnp.concatenate([cos_c, cos_l], axis=0), jnp.concatenate([sin_c, sin_l], axis=0)


def _mla_prep(p32, cos, sin, gq, gkv, wq, wqr, wk, wv):
    B = p32.shape[0]
    full = lambda a: pl.BlockSpec(a.shape, lambda b, t: (0,) * a.ndim)
    rows = lambda width, blk: pl.BlockSpec((None, ROW_TILE, width), lambda b, t, blk=blk: (b, t, blk))
    tab = pl.BlockSpec((ROW_TILE, MLA_HEAD_PAD), lambda b, t: (t, 0))
    return pl.pallas_call(
        _mla_prep_body,
        grid=(B, N_ROW_TILES),
        in_specs=[rows(MLA_Q_RANK, P32_CQ // MLA_Q_RANK), rows(MLA_KV_RANK, P32_CKV // MLA_KV_RANK),
                  rows(LANES, P32_KR // LANES), rows(LANES, P32_KRP // LANES), tab, tab,
                  full(gq), full(gkv), full(wq), full(wqr), full(wk), full(wv)],
        out_specs=[rows(MLA_QK_WIDTH, 0), rows(MLA_QK_WIDTH, 0),
                   pl.BlockSpec((None, MLA_WIDTH, ROW_TILE), lambda b, t: (b, 0, t))],
        out_shape=[jax.ShapeDtypeStruct((B, TOK, MLA_QK_WIDTH), jnp.bfloat16),
                   jax.ShapeDtypeStruct((B, TOK, MLA_QK_WIDTH), jnp.bfloat16),
                   jax.ShapeDtypeStruct((B, MLA_WIDTH, TOK), jnp.bfloat16)],
        compiler_params=_cparams(2),
        name="mla_prep",
    )(p32, p32, p32, p32, cos, sin, gq, gkv, wq, wqr, wk, wv)


def _mla_attn_body(q_ref, k_ref, vt_ref, o_ref, sa_s, sb_s, ma_s, mb_s):
    heads = [slice(hh * MLA_HEAD_PAD, (hh + 1) * MLA_HEAD_PAD) for hh in range(2)]

    def over_keys(x, op):
        slabs = op(x.reshape(x.shape[0] // MLA_KEY_SLAB, MLA_KEY_SLAB, x.shape[1]), axis=0)
        return op(slabs, axis=0, keepdims=True)

    def qrows_of(j):
        return pl.ds(pl.multiple_of(CTX_LEN + j * MLA_TQ, MLA_TQ), MLA_TQ)

    def scores(j, s_s, m_s):
        for hh in range(2):
            st = _dot_nt(k_ref[:, heads[hh]], q_ref[qrows_of(j), heads[hh]])
            s_s[hh] = st
            m_s[hh, 0:1, :] = over_keys(st, jnp.max)

    def finish(st_of, m_of, qrows, n_keys):
        outs = []
        for hh in range(2):
            p = jnp.exp2(st_of(hh) - m_of(hh))
            den = over_keys(p, jnp.sum)
            ot = _dot(vt_ref[hh * MLA_V:(hh + 1) * MLA_V, 0:n_keys], p.astype(jnp.bfloat16))
            outs.append(ot / den)
        o_ref[qrows, :] = jnp.concatenate(outs, axis=0).T.astype(o_ref.dtype)

    def finish_block(j, s_s, m_s):
        finish(lambda hh: s_s[hh], lambda hh: m_s[hh, 0:1, :], qrows_of(j), TOK)

    n_blocks = SEQ // MLA_TQ
    scores(0, sa_s, ma_s)

    def pair_body(t, carry):
        finish_block(2 * t, sa_s, ma_s)
        scores(2 * t + 1, sb_s, mb_s)
        finish_block(2 * t + 1, sb_s, mb_s)
        scores(jnp.minimum(2 * t + 2, n_blocks - 1), sa_s, ma_s)
        return carry

    lax.fori_loop(0, n_blocks // 2, pair_body, 0)

    ctx_st = [_dot_nt(k_ref[0:CTX_LEN, heads[hh]], q_ref[0:CTX_LEN, heads[hh]]) for hh in range(2)]
    finish(lambda hh: ctx_st[hh], lambda hh: over_keys(ctx_st[hh], jnp.max), slice(0, CTX_LEN), CTX_LEN)


def _mla_attention(q, k, v):
    B = q.shape[0]
    qk = pl.BlockSpec((None, TOK, 2 * MLA_HEAD_PAD), lambda b, hp: (b, 0, hp))
    return pl.pallas_call(
        _mla_attn_body,
        grid=(B, MLA_WIDTH // LANES),
        in_specs=[qk, qk, pl.BlockSpec((None, LANES, TOK), lambda b, hp: (b, hp, 0))],
        out_specs=pl.BlockSpec((None, TOK, LANES), lambda b, hp: (b, 0, hp)),
        out_shape=jax.ShapeDtypeStruct((B, TOK, MLA_WIDTH), jnp.bfloat16),
        scratch_shapes=[pltpu.VMEM((2, TOK, MLA_TQ), jnp.float32)] * 2
        + [pltpu.VMEM((2, SUBLANES, MLA_TQ), jnp.float32)] * 2,
        compiler_params=_cparams(2),
        name="mla_attention",
    )(q, k, v)


def _out_proj_body(hg_ref, ml_ref, na_ref, x_ref, mod_ref, w_ref, gf_ref, wr_ref, o_ref, h_ref, aff_ref):
    o = (_dot(hg_ref[...], w_ref[0:HG_WIDTH, :])
         + _dot(ml_ref[...], w_ref[HG_WIDTH:HG_WIDTH + MLA_WIDTH, :])
         + _dot(na_ref[...], w_ref[HG_WIDTH + MLA_WIDTH:, :]))
    x = x_ref[...] + _mod_row(mod_ref, MOD_GATE_A) * o
    o_ref[...] = x
    h = _modulated_norm(x, gf_ref[...], _mod_row(mod_ref, MOD_SHIFT_F), _mod_row(mod_ref, MOD_SCALE_F))
    h_ref[...] = h
    logits = _dot(h.astype(jnp.bfloat16), wr_ref[...])
    lane = lax.broadcasted_iota(jnp.int32, logits.shape, 1)
    logits = jnp.where(lane < N_EXPERTS, logits, NA_MASKED)
    e = jnp.exp(logits - jnp.max(logits, axis=-1, keepdims=True))
    aff = e / jnp.sum(e, axis=-1, keepdims=True)
    aff_ref[...] = aff.T[0:N_EXPERTS, :]


def _out_proj(hg, ml, na, xc, mods, w, g_ffn, w_router):
    B = xc.shape[0]
    return pl.pallas_call(
        _out_proj_body,
        grid=(B, N_ROW_TILES),
        in_specs=[_row_spec(HG_WIDTH), _row_spec(MLA_WIDTH), _row_spec(NA_WIDTH), _row_spec(D_MODEL),
                  _mod_spec(),
                  pl.BlockSpec((D_MIX, D_MODEL), lambda b, t: (0, 0)),
                  pl.BlockSpec((1, D_MODEL), lambda b, t: (0, 0)),
                  pl.BlockSpec((D_MODEL, LANES), lambda b, t: (0, 0))],
        out_specs=[_row_spec(D_MODEL), _row_spec(D_MODEL),
                   pl.BlockSpec((None, N_EXPERTS, ROW_TILE), lambda b, t: (b, 0, t))],
        out_shape=[jax.ShapeDtypeStruct((B, TOK, D_MODEL), jnp.float32),
                   jax.ShapeDtypeStruct((B, TOK, D_MODEL), jnp.float32),
                   jax.ShapeDtypeStruct((B, N_EXPERTS, TOK), jnp.float32)],
        compiler_params=_cparams(2),
        name="out_proj",
    )(hg, ml, na, xc, mods, w, g_ffn, w_router)


CAP_CTX = EC_CAPACITY * CTX_LEN // N_EXPERTS
CAP_LAT = EC_CAPACITY * SEQ // N_EXPERTS
CAP = CAP_CTX + CAP_LAT
IDX_PAD = -(-CAP // LANES) * LANES
ONE_BITS = 0x3F800000
ROUTE_ROWS = -(-(TOK // LANES * N_EXPERTS) // LANES) * LANES
ROUTE_SPLIT = 32
assert CAP < 2 * ROUTE_SPLIT ** 2


def _prefix_count_blocks(m, upper):
    outs, off = [], jnp.zeros((m.shape[0], 1), jnp.float32)
    for j in range(m.shape[1] // LANES):
        loc = _dot(m[:, j * LANES:(j + 1) * LANES].astype(jnp.bfloat16), upper)
        outs.append(loc + off)
        off = off + loc[:, LANES - 1:LANES]
    return outs


def _prefix_count(m, upper):
    return jnp.concatenate(_prefix_count_blocks(m, upper), axis=1)


def _select_top(a, k, upper):
    rows = a.shape[0]

    def step(_, lohi):
        lo, hi = lohi
        mid = lo + ((hi - lo + 1) >> 1)
        cnt = jnp.sum((a >= pltpu.bitcast(mid, jnp.float32)).astype(jnp.int32), axis=-1, keepdims=True)
        ok = cnt >= k
        return jnp.where(ok, mid, lo), jnp.where(ok, hi, mid - 1)

    thr, _ = lax.fori_loop(0, 31, step, (jnp.zeros((rows, 1), jnp.int32), jnp.full((rows, 1), ONE_BITS, jnp.int32)))
    thr = pltpu.bitcast(thr, jnp.float32)
    gt = (a > thr).astype(jnp.float32)
    eq = (a == thr).astype(jnp.float32)
    need = k - jnp.sum(gt, axis=-1, keepdims=True)
    return gt + eq * (_prefix_count(eq, upper) <= need).astype(jnp.float32)


def _route_body(aff_ref, idx_ref):
    a = aff_ref[...]
    r = lax.broadcasted_iota(jnp.int32, (LANES, LANES), 0)
    c = lax.broadcasted_iota(jnp.int32, (LANES, LANES), 1)
    upper = (r <= c).astype(jnp.bfloat16)
    sel = jnp.concatenate([_select_top(a[:, :CTX_LEN], CAP_CTX, upper),
                           _select_top(a[:, CTX_LEN:], CAP_LAT, upper)], axis=1)
    blocks = _prefix_count_blocks(sel, upper)
    pos_rows = jnp.concatenate(blocks + [jnp.zeros((ROUTE_ROWS - len(blocks) * N_EXPERTS, LANES), jnp.float32)],
                               axis=0)
    pos_hi = jnp.floor(pos_rows * (1.0 / ROUTE_SPLIT))
    pos_lo = pos_rows - pos_hi * ROUTE_SPLIT
    pos_digits = jnp.concatenate([pos_hi, pos_lo], axis=1).astype(jnp.bfloat16)
    tok_block = lax.broadcasted_iota(jnp.int32, (TOK, LANES), 0) // LANES
    block_cum = _dot(sel.astype(jnp.bfloat16),
                     (tok_block <= lax.broadcasted_iota(jnp.int32, (TOK, LANES), 1)).astype(jnp.bfloat16))
    lane = lax.broadcasted_iota(jnp.int32, (LANES, LANES), 1)
    row_id = lax.broadcasted_iota(jnp.int32, (LANES, ROUTE_ROWS), 1)
    for sb in range(IDX_PAD // LANES):
        slot = (lax.broadcasted_iota(jnp.int32, (LANES, 1), 0) + sb * LANES).astype(jnp.float32)
        n_before = [jnp.sum((block_cum[e:e + 1, :] <= slot).astype(jnp.float32), axis=-1, keepdims=True)
                    for e in range(N_EXPERTS)]
        pick = jnp.concatenate([(row_id == n_before[e].astype(jnp.int32) * N_EXPERTS + e).astype(jnp.bfloat16)
                                for e in range(N_EXPERTS)], axis=0)
        digits = _dot(pick, pos_digits)
        pos_blk = digits[:, :LANES] * ROUTE_SPLIT + digits[:, LANES:]
        cols = jnp.zeros((LANES, LANES), jnp.float32)
        for e in range(N_EXPERTS):
            inside = jnp.sum((pos_blk[e * LANES:(e + 1) * LANES, :] <= slot).astype(jnp.float32),
                             axis=-1, keepdims=True)
            cols = jnp.where(lane == e, n_before[e] * LANES + inside, cols)
        idx = jnp.minimum(cols.T[0:N_EXPERTS, :], TOK - 1)
        idx_ref[:, sb * LANES:(sb + 1) * LANES] = idx.astype(jnp.int32)


def _route(aff):
    B = aff.shape[0]
    return pl.pallas_call(
        _route_body,
        grid=(B,),
        in_specs=[pl.BlockSpec((None, N_EXPERTS, TOK), lambda b: (b, 0, 0))],
        out_specs=pl.BlockSpec((None, N_EXPERTS, IDX_PAD), lambda b: (b, 0, 0)),
        out_shape=jax.ShapeDtypeStruct((B, N_EXPERTS, IDX_PAD), jnp.int32),
        compiler_params=_cparams(1),
        name="route",
    )(aff)


SCATTER_BATCH = 4


def _smem_row(n):
    return pl.BlockSpec((None, None, 1, n), lambda b, e: (b, e, 0, 0), memory_space=pltpu.SMEM)


def _slot_spec():
    return pl.BlockSpec((None, None, CAP, D_MODEL), lambda b, e: (b, e, 0, 0))


def _gather_body(idx_ref, h_ref, xs_ref, xs_s):
    for s in range(CAP):
        xs_s[s:s + 1, :] = h_ref[pl.ds(idx_ref[0, s], 1), :]
    xs_ref[...] = xs_s[...].astype(xs_ref.dtype)


def _swiglu_body(xs_ref, w1_ref, w3_ref, w2_ref, y_ref):
    xs = xs_ref[...]
    a = _dot(xs, w1_ref[...])
    u = _dot(xs, w3_ref[...])
    y_ref[...] = _dot((a * jax.nn.sigmoid(a) * u).astype(jnp.bfloat16), w2_ref[...])


def _scatter_body(idx_ref, aff_ref, y_ref, acc_ref):
    @pl.when(pl.program_id(1) == 0)
    def _():
        acc_ref[...] = jnp.zeros_like(acc_ref)

    for s0 in range(0, CAP, SCATTER_BATCH):
        rows = [idx_ref[0, s0 + j] for j in range(SCATTER_BATCH)]
        new = [acc_ref[pl.ds(rows[j], 1), :] + aff_ref[0, rows[j]] * y_ref[s0 + j:s0 + j + 1, :]
               for j in range(SCATTER_BATCH)]
        for j in range(SCATTER_BATCH):
            acc_ref[pl.ds(rows[j], 1), :] = new[j]


def _experts(idx, aff, h, w1, w3, w2):
    B = h.shape[0]
    grid = (B, N_EXPERTS)
    idx4, aff4 = idx[:, :, None, :], aff[:, :, None, :]
    sample = pl.BlockSpec((None, TOK, D_MODEL), lambda b, e: (b, 0, 0))
    wspec = lambda k, n: pl.BlockSpec((None, k, n), lambda b, e: (e, 0, 0))
    slots = lambda dtype: jax.ShapeDtypeStruct((B, N_EXPERTS, CAP, D_MODEL), dtype)
    xs = pl.pallas_call(
        _gather_body, grid=grid,
        in_specs=[_smem_row(IDX_PAD), sample],
        out_specs=_slot_spec(), out_shape=slots(jnp.bfloat16),
        scratch_shapes=[pltpu.VMEM((CAP, D_MODEL), jnp.float32)],
        compiler_params=_cparams(2), name="expert_gather",
    )(idx4, h)
    y = pl.pallas_call(
        _swiglu_body, grid=grid,
        in_specs=[_slot_spec(), wspec(D_MODEL, F_EXPERT), wspec(D_MODEL, F_EXPERT), wspec(F_EXPERT, D_MODEL)],
        out_specs=_slot_spec(), out_shape=slots(jnp.float32),
        compiler_params=_cparams(2), name="expert_swiglu",
    )(xs, w1, w3, w2)
    return pl.pallas_call(
        _scatter_body, grid=grid,
        in_specs=[_smem_row(IDX_PAD), _smem_row(TOK), _slot_spec()],
        out_specs=sample, out_shape=jax.ShapeDtypeStruct((B, TOK, D_MODEL), jnp.float32),
        compiler_params=_cparams(2), name="expert_scatter",
    )(idx4, aff4, y)


def _final_body(x_ref, delta_ref, mod_ref, g_ref, o_ref):
    x = x_ref[...] + _mod_row(mod_ref, MOD_GATE_F) * delta_ref[...]
    o_ref[...] = x * lax.rsqrt(jnp.mean(x * x, axis=-1, keepdims=True) + EPS) * g_ref[...]


def _final_norm(xc, delta, mods, g):
    B = xc.shape[0]
    lat_rows = pl.BlockSpec((None, ROW_TILE, D_MODEL), lambda b, t: (b, t + CTX_TILES, 0))
    return pl.pallas_call(
        _final_body,
        grid=(B, SEQ // ROW_TILE),
        in_specs=[lat_rows, lat_rows,
                  pl.BlockSpec((None, None, 6, D_MODEL), lambda b, t: (b, 1, 0, 0)),
                  pl.BlockSpec((1, D_MODEL), lambda b, t: (0, 0))],
        out_specs=pl.BlockSpec((None, ROW_TILE, D_MODEL), lambda b, t: (b, t, 0)),
        out_shape=jax.ShapeDtypeStruct((B, SEQ, D_MODEL), jnp.float32),
        compiler_params=_cparams(2),
        name="final_norm",
    )(xc, delta, mods, g)


def kernel(x, c, ctx, c_ctx, ada_w, ada_b, norm_mix_g, norm_ffn_g, w_in, hgrn_lb_logits,
           hgrn_onorm_g, mla_qnorm_g, mla_w_uq, mla_kvnorm_g, mla_w_ukv, na_rpb, w_out,
           router_w, exp_w1, exp_w3, exp_w2, final_norm_g):
    B = x.shape[0]
    lb_w = jax.nn.softmax(hgrn_lb_logits.astype(jnp.float32), axis=1)
    lb_all = jnp.cumsum(lb_w, axis=1) - lb_w[:, :1]
    s_lat = jax.nn.silu(c)
    s_ctx = jax.nn.silu(c_ctx)
    w_in_b = _widen_w_in(w_in)
    w_out_b = w_out.astype(jnp.bfloat16)
    wq, wqr, wk, wv = _mla_weights(mla_w_uq, mla_w_ukv)
    cos, sin = _rope_tables()
    w_router = jnp.pad(router_w, ((0, 0), (0, 0), (0, LANES - N_EXPERTS))).astype(jnp.bfloat16)
    w1_b, w3_b, w2_b = (w.astype(jnp.bfloat16) for w in (exp_w1, exp_w3, exp_w2))
    row = lambda g: g.reshape(1, -1).astype(jnp.float32)
    xc = jnp.concatenate([ctx, x], axis=1)
    delta, prev_mods = None, None
    for layer in range(DEPTH):
        mod_l = (s_lat @ ada_w[layer] + ada_b[layer]).reshape(B, 6, D_MODEL)
        mod_c = jnp.broadcast_to((s_ctx @ ada_w[layer] + ada_b[layer]).reshape(1, 6, D_MODEL), (B, 6, D_MODEL))
        mods = jnp.stack([mod_c, mod_l], axis=1)

        xc, p32, p16 = _in_proj(xc, delta, prev_mods, row(norm_mix_g[layer]), mods, w_in_b[layer])
        hg = _hgrn_mixer(p32, lb_all[:, layer], hgrn_onorm_g[layer])
        q, k, v = _mla_prep(p32, cos, sin, row(mla_qnorm_g[layer]), row(mla_kvnorm_g[layer]),
                            wq[layer], wqr[layer], wk[layer], wv[layer])
        ml = _mla_attention(q, k, v)
        na = _na_attention(p16, _na_bias_tables(na_rpb[layer]))
        xc, h, aff = _out_proj(hg, ml, na, xc, mods, w_out_b[layer], row(norm_ffn_g[layer]), w_router[layer])
        delta = _experts(_route(aff), aff, h, w1_b[layer], w3_b[layer], w2_b[layer])
        prev_mods = mods
    return _final_norm(xc, delta, prev_mods, row(final_norm_g))
```
